```python
import math
import jax, jax.numpy as jnp
from jax import lax
import numpy as np

D_MODEL = 1024
BATCH = 8
SEQ = 2048
DEPTH = 4
DEC_BATCH = 128
DEC_SEQ = 8
PAST_LEN = 2048
PAGE_SIZE = 128

N_MIXERS = 3
N_ATTN = (DEPTH + 2) // 3
N_CONV = (DEPTH + 1) // 3
N_DELTA = DEPTH // 3

ATT_HEADS = 8
ATT_DH = 64
ATT_VD = 2 * ATT_DH
ATT_WIDTH = ATT_HEADS * ATT_VD
ROPE_THETA = 10000.0
Q_BLOCK = 128
NEG_INF = -1e30

CONV_CH = D_MODEL
CONV_WIDTH = 31

DN_HEADS = 8
DN_DK = 128
DN_DV = 128
DN_KW = DN_HEADS * DN_DK
DN_VW = DN_HEADS * DN_DV
DN_QKV = 2 * DN_KW + DN_VW
DN_CONV = 4
DN_CHUNK = 64

EPS = 1e-6

kernel_name = "diffattn_conformer_gdn_hybrid_step"


def _rmsnorm(x, g):
    x32 = x.astype(jnp.float32)
    y = x32 * lax.rsqrt(jnp.mean(x32 * x32, axis=-1, keepdims=True) + EPS)
    return (y * g.astype(jnp.float32)).astype(x.dtype)


def _layernorm(x, g, b):
    x32 = x.astype(jnp.float32)
    xc = x32 - jnp.mean(x32, axis=-1, keepdims=True)
    var = jnp.mean(xc * xc, axis=-1, keepdims=True)
    y = xc * lax.rsqrt(var + EPS) * g.astype(jnp.float32) + b.astype(jnp.float32)
    return y.astype(x.dtype)


def _l2norm(x):
    return x * lax.rsqrt(jnp.sum(x * x, axis=-1, keepdims=True) + EPS)


def _rope(x, pos):
    inv = ROPE_THETA ** (-jnp.arange(0, ATT_DH, 2, dtype=jnp.float32) / ATT_DH)
    ang = pos.astype(jnp.float32)[:, None] * inv[None, :]
    ang = jnp.concatenate([ang, ang], axis=-1)
    cos = jnp.cos(ang)[None, :, None, None, :]
    sin = jnp.sin(ang)[None, :, None, None, :]
    x32 = x.astype(jnp.float32)
    half = ATT_DH // 2
    rot = jnp.concatenate([-x32[..., half:], x32[..., :half]], axis=-1)
    return (x32 * cos + rot * sin).astype(x.dtype)


def _causal_dwconv(xc, w):
    return lax.conv_general_dilated(
        xc, w[:, None, :].astype(xc.dtype), (1,), 'VALID',
        dimension_numbers=('NWC', 'WIO', 'NWC'), feature_group_count=xc.shape[-1])


def _diff_lambda(lq, lk, lam_init):
    lq = lq.astype(jnp.float32)
    lk = lk.astype(jnp.float32)
    return jnp.exp(jnp.sum(lq[0] * lk[0])) - jnp.exp(jnp.sum(lq[1] * lk[1])) + lam_init


def _diff_attn_project(h, pos, w_in, qn, kn):
    B, T, _ = h.shape
    q, k, v, z = jnp.split(h @ w_in, 4, axis=-1)
    q = _rope(_rmsnorm(q.reshape(B, T, ATT_HEADS, 2, ATT_DH), qn), pos)
    k = _rope(_rmsnorm(k.reshape(B, T, ATT_HEADS, 2, ATT_DH), kn), pos)
    v = v.reshape(B, T, ATT_HEADS, ATT_VD)
    return q, k, v, z


def _diff_combine(s, lam):
    p = jax.nn.softmax(s, axis=-1)
    return p[:, :, 0] - lam * p[:, :, 1]


def _diff_attn_out(o, z, lam_init, subln, w_out):
    B, T = o.shape[:2]
    o = _rmsnorm(o, subln) * (1.0 - lam_init)
    return (o.reshape(B, T, ATT_WIDTH) * jax.nn.silu(z)) @ w_out


def _diff_attn_prompt(h, lam, lam_init, w_in, qn, kn, subln, w_out):
    B, T, _ = h.shape
    pos = jnp.arange(T)
    q, k, v, z = _diff_attn_project(h, pos, w_in, qn, kn)
    nb = T // Q_BLOCK
    qb = jnp.moveaxis(q.reshape(B, nb, Q_BLOCK, ATT_HEADS, 2, ATT_DH), 1, 0)
    scale = ATT_DH ** -0.5

    def block(args):
        qi, bi = args
        s = jnp.einsum('bqhcd,bshcd->bhcqs', qi, k).astype(jnp.float32) * scale
        qpos = bi * Q_BLOCK + jnp.arange(Q_BLOCK)
        s = jnp.where(qpos[:, None] >= pos[None, :], s, NEG_INF)
        w = _diff_combine(s, lam)
        return jnp.einsum('bhqs,bshe->bqhe', w.astype(v.dtype), v)

    o = lax.map(block, (qb, jnp.arange(nb)))
    o = jnp.moveaxis(o, 0, 1).reshape(B, T, ATT_HEADS, ATT_VD)
    return _diff_attn_out(o, z, lam_init, subln, w_out), k, v


def _diff_attn_sample(h, cache_k, cache_v, li, page_table, lam, lam_init, w_in, qn, kn, subln, w_out):
    B, T, _ = h.shape
    past = page_table.shape[1] * PAGE_SIZE
    pos = past + jnp.arange(T)
    q, k, v, z = _diff_attn_project(h, pos, w_in, qn, kn)
    kp = cache_k[li, page_table].reshape(B, past, ATT_HEADS, 2, ATT_DH)
    vp = cache_v[li, page_table].reshape(B, past, ATT_HEADS, ATT_VD)
    scale = ATT_DH ** -0.5
    s_past = jnp.einsum('bthcd,bshcd->bhcts', q, kp).astype(jnp.float32) * scale
    s_new = jnp.einsum('bthcd,bshcd->bhcts', q, k).astype(jnp.float32) * scale
    causal = jnp.arange(T)[:, None] >= jnp.arange(T)[None, :]
    s_new = jnp.where(causal, s_new, NEG_INF)
    w = _diff_combine(jnp.concatenate([s_past, s_new], axis=-1), lam).astype(v.dtype)
    o = (jnp.einsum('bhts,bshe->bthe', w[..., :past], vp)
         + jnp.einsum('bhts,bshe->bthe', w[..., past:], v))
    return _diff_attn_out(o, z, lam_init, subln, w_out), k, v


def _conformer_conv_mixer(h, buf, w_in, dw_w, dw_b, ln_g, ln_b, w_out):
    a, b, z = jnp.split(h @ w_in, 3, axis=-1)
    u = a * jax.nn.sigmoid(b)
    uc = jnp.concatenate([buf.astype(u.dtype), u], axis=1)
    new_buf = uc[:, uc.shape[1] - (CONV_WIDTH - 1):]
    c = _causal_dwconv(uc, dw_w) + dw_b
    c = jax.nn.silu(_layernorm(c, ln_g, ln_b))
    return (c * jax.nn.silu(z)) @ w_out, new_buf


def _chunk_gated_delta(q, k, v, g, beta, S0):
    B, T, H, DK = q.shape
    DV = v.shape[-1]
    L = DN_CHUNK
    pad = (-T) % L

    def padt(x):
        return jnp.pad(x, [(0, 0), (0, pad)] + [(0, 0)] * (x.ndim - 2))

    q, k, v, g, beta = padt(q), padt(k), padt(v), padt(g), padt(beta)
    N = (T + pad) // L

    def chunks(x):
        x = x.reshape((B, N, L, H) + x.shape[3:])
        return jnp.moveaxis(x, (1, 3), (0, 2))

    qc, kc, vc, gc, bc = chunks(q), chunks(k), chunks(v), chunks(g), chunks(beta)
    gc = jnp.cumsum(gc, axis=-1)
    kb = kc * bc[..., None]
    vb = vc * bc[..., None]
    incl = jnp.tril(jnp.ones((L, L), dtype=bool))
    strict = jnp.tril(jnp.ones((L, L), dtype=bool), -1)
    gdiff = gc[..., :, None] - gc[..., None, :]
    decay = jnp.where(incl, jnp.exp(jnp.where(incl, gdiff, 0.0)), 0.0)
    m = jnp.where(strict, jnp.einsum('nbhid,nbhjd->nbhij', kb, kc) * decay, 0.0)
    eye = jnp.eye(L, dtype=jnp.float32)
    tinv = lax.linalg.triangular_solve(m + eye, jnp.broadcast_to(eye, m.shape),
                                       left_side=True, lower=True, unit_diagonal=True)
    u = tinv @ vb
    w = tinv @ (kb * jnp.exp(gc)[..., None])
    qk = jnp.einsum('nbhid,nbhjd->nbhij', qc, kc) * decay

    def step(S, xs):
        qi, ki, ui, wi, gi, qki = xs
        v_new = ui - jnp.einsum('bhid,bhde->bhie', wi, S)
        o = (jnp.einsum('bhid,bhde->bhie', qi * jnp.exp(gi)[..., None], S)
             + jnp.einsum('bhij,bhje->bhie', qki, v_new))
        glast = gi[..., -1:]
        S = (S * jnp.exp(glast)[..., None]
             + jnp.einsum('bhid,bhie->bhde', ki * jnp.exp(glast - gi)[..., None], v_new))
        return S, o

    S, o = lax.scan(step, S0, (qc, kc, u, w, gc, qk))
    o = jnp.moveaxis(o, (0, 2), (1, 3)).reshape(B, N * L, H, DV)[:, :T]
    return o, S


def _gated_delta_mixer(h, conv_buf, S0, w_in, conv_w, a_log, dt_bias, o_norm, w_out):
    B, T, _ = h.shape
    proj = h @ w_in
    qkv = proj[..., :DN_QKV]
    z = proj[..., DN_QKV:DN_QKV + DN_VW]
    a = proj[..., DN_QKV + DN_VW:DN_QKV + DN_VW + DN_HEADS]
    b = proj[..., DN_QKV + DN_VW + DN_HEADS:]
    xc = jnp.concatenate([conv_buf.astype(qkv.dtype), qkv], axis=1)
    new_buf = xc[:, xc.shape[1] - (DN_CONV - 1):]
    qkv = jax.nn.silu(_causal_dwconv(xc, conv_w))
    q = qkv[..., :DN_KW].reshape(B, T, DN_HEADS, DN_DK).astype(jnp.float32)
    k = qkv[..., DN_KW:2 * DN_KW].reshape(B, T, DN_HEADS, DN_DK).astype(jnp.float32)
    v = qkv[..., 2 * DN_KW:].reshape(B, T, DN_HEADS, DN_DV).astype(jnp.float32)
    q = _l2norm(q) * (DN_DK ** -0.5)
    k = _l2norm(k)
    beta = jax.nn.sigmoid(b.astype(jnp.float32))
    g = -jnp.exp(a_log.astype(jnp.float32)) * jax.nn.softplus(a.astype(jnp.float32) + dt_bias.astype(jnp.float32))
    o, S = _chunk_gated_delta(q, k, v, g, beta, S0.astype(jnp.float32))
    o = _rmsnorm(o.astype(h.dtype), o_norm).reshape(B, T, DN_VW) * jax.nn.silu(z)
    return o @ w_out, new_buf, S.astype(S0.dtype)


def setup_inputs(seed: int = 0) -> dict:
    key = jax.random.key(seed)
    ks = iter(jax.random.split(key, 32))

    def nrm(shape, scale):
        return jax.random.normal(next(ks), shape, jnp.float32) * scale

    n_pages = PAST_LEN // PAGE_SIZE
    n_used = DEC_BATCH * n_pages
    n_pool = n_used + max(1, n_used // 4)
    page_table = jax.random.permutation(next(ks), n_pool)[:n_used].reshape(DEC_BATCH, n_pages).astype(jnp.int32)

    dt = jnp.exp(jax.random.uniform(next(ks), (N_DELTA, DN_HEADS), jnp.float32, math.log(1e-3), math.log(1e-1)))
    dt_bias = dt + jnp.log(-jnp.expm1(-dt))
    a_log = jnp.log(jax.random.uniform(next(ks), (N_DELTA, DN_HEADS), jnp.float32, 1.0, 16.0))

    return {
        'x_prompt': nrm((BATCH, SEQ, D_MODEL), 1.0),
        'x_sample': nrm((DEC_BATCH, DEC_SEQ, D_MODEL), 1.0),
        'cache_k': nrm((N_ATTN, n_pool, PAGE_SIZE, ATT_HEADS, 2, ATT_DH), 1.0),
        'cache_v': nrm((N_ATTN, n_pool, PAGE_SIZE, ATT_HEADS, ATT_VD), 1.0),
        'page_table': page_table,
        'state_conv': nrm((N_CONV, DEC_BATCH, CONV_WIDTH - 1, CONV_CH), 0.5),
        'state_delta_conv': nrm((N_DELTA, DEC_BATCH, DN_CONV - 1, DN_QKV), 1.0),
        'state_delta_S': nrm((N_DELTA, DEC_BATCH, DN_HEADS, DN_DK, DN_DV), 0.1),
        'norm_gain': 1.0 + nrm((DEPTH, D_MODEL), 0.01),
        'attn_w_in': nrm((N_ATTN, D_MODEL, 4 * ATT_WIDTH), D_MODEL ** -0.5),
        'attn_q_norm': 1.0 + nrm((N_ATTN, ATT_DH), 0.01),
        'attn_k_norm': 1.0 + nrm((N_ATTN, ATT_DH), 0.01),
        'attn_lambda_q': nrm((N_ATTN, 2, ATT_DH), 0.1),
        'attn_lambda_k': nrm((N_ATTN, 2, ATT_DH), 0.1),
        'attn_subln': 1.0 + nrm((N_ATTN, ATT_VD), 0.01),
        'attn_w_out': nrm((N_ATTN, ATT_WIDTH, D_MODEL), ATT_WIDTH ** -0.5),
        'conv_w_in': nrm((N_CONV, D_MODEL, 3 * CONV_CH), D_MODEL ** -0.5),
        'conv_dw_w': nrm((N_CONV, CONV_WIDTH, CONV_CH), CONV_WIDTH ** -0.5),
        'conv_dw_b': nrm((N_CONV, CONV_CH), 0.02),
        'conv_ln_g': 1.0 + nrm((N_CONV, CONV_CH), 0.01),
        'conv_ln_b': nrm((N_CONV, CONV_CH), 0.02),
        'conv_w_out': nrm((N_CONV, CONV_CH, D_MODEL), CONV_CH ** -0.5),
        'delta_w_in': nrm((N_DELTA, D_MODEL, DN_QKV + DN_VW + 2 * DN_HEADS), D_MODEL ** -0.5),
        'delta_conv_w': nrm((N_DELTA, DN_CONV, DN_QKV), DN_CONV ** -0.5),
        'delta_a_log': a_log,
        'delta_dt_bias': dt_bias,
        'delta_o_norm': 1.0 + nrm((N_DELTA, DN_DV), 0.01),
        'delta_w_out': nrm((N_DELTA, DN_VW, D_MODEL), DN_VW ** -0.5),
    }


def reference(x_prompt, x_sample, cache_k, cache_v, page_table, state_conv, state_delta_conv, state_delta_S,
              norm_gain, attn_w_in, attn_q_norm, attn_k_norm, attn_lambda_q, attn_lambda_k, attn_subln, attn_w_out,
              conv_w_in, conv_dw_w, conv_dw_b, conv_ln_g, conv_ln_b, conv_w_out,
              delta_w_in, delta_conv_w, delta_a_log, delta_dt_bias, delta_o_norm, delta_w_out):
    yp, ys = x_prompt, x_sample
    Bp = x_prompt.shape[0]
    kp_l, vp_l, ks_l, vs_l = [], [], [], []
    cp_l, cs_l = [], []
    dcp_l, dcs_l, dSp_l, dSs_l = [], [], [], []
    for i in range(DEPTH):
        kind = i % N_MIXERS
        j = i // N_MIXERS
        hp = _rmsnorm(yp, norm_gain[i])
        hs = _rmsnorm(ys, norm_gain[i])
        if kind == 0:
            lam_init = 0.8 - 0.6 * math.exp(-0.3 * i)
            lam = _diff_lambda(attn_lambda_q[j], attn_lambda_k[j], lam_init)
            op, kp, vp = _diff_attn_prompt(hp, lam, lam_init, attn_w_in[j], attn_q_norm[j], attn_k_norm[j],
                                           attn_subln[j], attn_w_out[j])
            os_, ks, vs = _diff_attn_sample(hs, cache_k, cache_v, j, page_table, lam, lam_init, attn_w_in[j],
                                            attn_q_norm[j], attn_k_norm[j], attn_subln[j], attn_w_out[j])
            kp_l.append(kp)
            vp_l.append(vp)
            ks_l.append(ks)
            vs_l.append(vs)
        elif kind == 1:
            zero_buf = jnp.zeros((Bp, CONV_WIDTH - 1, CONV_CH), hp.dtype)
            op, bp = _conformer_conv_mixer(hp, zero_buf, conv_w_in[j], conv_dw_w[j], conv_dw_b[j],
                                           conv_ln_g[j], conv_ln_b[j], conv_w_out[j])
            os_, bs = _conformer_conv_mixer(hs, state_conv[j], conv_w_in[j], conv_dw_w[j], conv_dw_b[j],
                                            conv_ln_g[j], conv_ln_b[j], conv_w_out[j])
            cp_l.append(bp)
            cs_l.append(bs)
        else:
            zero_buf = jnp.zeros((Bp, DN_CONV - 1, DN_QKV), hp.dtype)
            zero_S = jnp.zeros((Bp, DN_HEADS, DN_DK, DN_DV), hp.dtype)
            op, bp, Sp = _gated_delta_mixer(hp, zero_buf, zero_S, delta_w_in[j], delta_conv_w[j], delta_a_log[j],
                                            delta_dt_bias[j], delta_o_norm[j], delta_w_out[j])
            os_, bs, Ss = _gated_delta_mixer(hs, state_delta_conv[j], state_delta_S[j], delta_w_in[j],
                                             delta_conv_w[j], delta_a_log[j], delta_dt_bias[j],
                                             delta_o_norm[j], delta_w_out[j])
            dcp_l.append(bp)
            dcs_l.append(bs)
            dSp_l.append(Sp)
            dSs_l.append(Ss)
        yp = yp + op
        ys = ys + os_
    k_prompt = jnp.stack(kp_l)
    v_prompt = jnp.stack(vp_l)
    k_sample = jnp.stack(ks_l)
    v_sample = jnp.stack(vs_l)
    conv_prompt = jnp.stack(cp_l)
    conv_sample = jnp.stack(cs_l)
    dconv_prompt = jnp.stack(dcp_l)
    dconv_sample = jnp.stack(dcs_l)
    dS_prompt = jnp.stack(dSp_l)
    dS_sample = jnp.stack(dSs_l)
    return (yp, ys, k_prompt, v_prompt, k_sample, v_sample, conv_prompt, conv_sample,
            dconv_prompt, dconv_sample, dS_prompt, dS_sample)
```

```python
import functools
import math

import jax
import jax.numpy as jnp
from jax import lax
from jax.experimental import pallas as pl
from jax.experimental.pallas import tpu as pltpu

F32 = jnp.float32
BF16 = jnp.bfloat16

D_MODEL = 1024
DEPTH = 4
N_MIXERS = 3
PAGE_SIZE = 128
ATT_HEADS = 8
ATT_DH = 64
ATT_VD = 128
ROPE_THETA = 10000.0
NEG_INF = -1e30
CONV_WIDTH = 31
DN_HEADS = 8
DN_DK = 128
DN_QKV = 3072
DN_CONV = 4
EPS = 1e-6

LANES = 128
ROW_TILE = 256
ATT_BLOCK = 512
CONV_TILE = 256
CONV_HALO = 32
DN_ROWS = 128
DN_HALO = 8
VMEM_LIMIT = 56 * 1024 * 1024


def _params(n_axes):
    return pltpu.CompilerParams(dimension_semantics=("arbitrary",) * n_axes,
                                vmem_limit_bytes=VMEM_LIMIT)


def _mm(a, b):
    return jnp.dot(a.astype(BF16), b.astype(BF16), preferred_element_type=F32)


def _mm_nt(a, b):
    return lax.dot_general(a.astype(BF16), b.astype(BF16), (((1,), (1,)), ((), ())),
                           preferred_element_type=F32)


def _mm_tn(a, b):
    return lax.dot_general(a.astype(BF16), b.astype(BF16), (((0,), (0,)), ((), ())),
                           preferred_element_type=F32)


def _mm_split(a, b):
    ah = a.astype(BF16)
    al = (a - ah.astype(F32)).astype(BF16)
    bh = b.astype(BF16)
    bl = (b - bh.astype(F32)).astype(BF16)
    dot = functools.partial(jnp.dot, preferred_element_type=F32)
    return dot(ah, bh) + dot(ah, bl) + dot(al, bh)


def _sigmoid(x):
    return 1.0 / (1.0 + jnp.exp(-x))


def _silu(x):
    return x * _sigmoid(x)


def _softplus(x):
    return jnp.maximum(x, 0.0) + jnp.log(1.0 + jnp.exp(-jnp.abs(x)))


def _rms_rows(x):
    return x * lax.rsqrt(jnp.mean(x * x, axis=-1, keepdims=True) + EPS)


def _norm_proj_kernel(x_ref, g_ref, w_ref, o_ref, *, col_chunk):
    h = (_rms_rows(x_ref[...]) * g_ref[...]).astype(BF16)
    n_out = o_ref.shape[1]
    for c0 in range(0, n_out, col_chunk):
        c1 = min(c0 + col_chunk, n_out)
        o_ref[:, c0:c1] = jnp.dot(h, w_ref[:, c0:c1], preferred_element_type=F32)


def _norm_proj(x, gain, w_bf16):
    n, n_out = x.shape[0], w_bf16.shape[1]
    return pl.pallas_call(
        functools.partial(_norm_proj_kernel, col_chunk=1024),
        grid=(n // ROW_TILE,),
        in_specs=[pl.BlockSpec((ROW_TILE, D_MODEL), lambda i: (i, 0)),
                  pl.BlockSpec((1, D_MODEL), lambda i: (0, 0)),
                  pl.BlockSpec((D_MODEL, n_out), lambda i: (0, 0))],
        out_specs=pl.BlockSpec((ROW_TILE, n_out), lambda i: (i, 0)),
        out_shape=jax.ShapeDtypeStruct((n, n_out), F32),
        compiler_params=_params(1),
        name="norm_proj",
    )(x, gain.reshape(1, D_MODEL), w_bf16)


def _attn_proj_kernel(x_ref, g_ref, w_ref, qn_ref, kn_ref, cos_ref, sin_ref,
                      q_ref, k_ref, v_ref, z_ref, kb_ref, vb_ref):
    h = (_rms_rows(x_ref[...]) * g_ref[...]).astype(BF16)
    width = ATT_HEADS * ATT_VD
    r = lax.broadcasted_iota(jnp.int32, (LANES, LANES), 0)
    c = lax.broadcasted_iota(jnp.int32, (LANES, LANES), 1)
    group_ones = ((r >> 6) == (c >> 6)).astype(BF16)
    lane = lax.broadcasted_iota(jnp.int32, (1, LANES), 1)
    first_half = (lane & (ATT_DH - 1)) < (ATT_DH // 2)
    cos = cos_ref[...]
    sin = sin_ref[...]

    def norm_rope(raw, gain):
        ms = jnp.dot((raw * raw).astype(BF16), group_ones, preferred_element_type=F32) * (1.0 / ATT_DH)
        xn = raw * lax.rsqrt(ms + EPS) * gain
        rot = jnp.where(first_half, pltpu.roll(xn, LANES - ATT_DH // 2, 1), pltpu.roll(xn, ATT_DH // 2, 1))
        return xn * cos + rot * sin

    pair = 2 * LANES
    for c0 in range(0, width, pair):
        q_raw = jnp.dot(h, w_ref[:, c0:c0 + pair], preferred_element_type=F32)
        k_raw = jnp.dot(h, w_ref[:, width + c0:width + c0 + pair], preferred_element_type=F32)
        for s in range(0, pair, LANES):
            sl = slice(c0 + s, c0 + s + LANES)
            q = norm_rope(q_raw[:, s:s + LANES], qn_ref[...]) * (ATT_DH ** -0.5)
            q_ref[:, sl] = q.astype(q_ref.dtype)
            k = norm_rope(k_raw[:, s:s + LANES], kn_ref[...])
            k_ref[:, sl] = k
            kb_ref[:, sl] = k.astype(BF16)
    v = jnp.dot(h, w_ref[:, 2 * width:3 * width], preferred_element_type=F32)
    v_ref[...] = v
    vb_ref[...] = v.astype(BF16)
    z_ref[...] = jnp.dot(h, w_ref[:, 3 * width:4 * width], preferred_element_type=F32)


def _attn_proj(x, gain, w_bf16, qn, kn, cos, sin, q_dtype):
    n = x.shape[0]
    n_tab = cos.shape[0] // ROW_TILE
    width = ATT_HEADS * ATT_VD
    row = lambda i: (i, 0)
    fixed = lambda i: (0, 0)
    out = lambda dt: jax.ShapeDtypeStruct((n, width), dt)
    return pl.pallas_call(
        _attn_proj_kernel,
        grid=(n // ROW_TILE,),
        in_specs=[pl.BlockSpec((ROW_TILE, D_MODEL), row),
                  pl.BlockSpec((1, D_MODEL), fixed),
                  pl.BlockSpec((D_MODEL, 4 * width), fixed),
                  pl.BlockSpec((1, LANES), fixed),
                  pl.BlockSpec((1, LANES), fixed),
                  pl.BlockSpec((ROW_TILE, LANES), lambda i: (i % n_tab, 0)),
                  pl.BlockSpec((ROW_TILE, LANES), lambda i: (i % n_tab, 0))],
        out_specs=[pl.BlockSpec((ROW_TILE, width), row)] * 6,
        out_shape=[out(q_dtype), out(F32), out(F32), out(F32), out(BF16), out(BF16)],
        compiler_params=_params(1),
        name="attn_proj",
    )(x, gain.reshape(1, D_MODEL), w_bf16, jnp.tile(qn, 2).reshape(1, LANES), jnp.tile(kn, 2).reshape(1, LANES),
      cos, sin)


def _rope_tables(pos):
    inv = ROPE_THETA ** (-jnp.arange(0, ATT_DH, 2, dtype=F32) / ATT_DH)
    ang = pos.astype(F32)[:, None] * inv[None, :]
    cos = jnp.cos(ang)
    sin = jnp.sin(ang)
    cos = jnp.concatenate([cos, cos, cos, cos], axis=-1)
    sin = jnp.concatenate([-sin, sin, -sin, sin], axis=-1)
    return cos, sin


def _diff_lambda_in_kernel(lq_ref, lk_ref, lam_init):
    e = jnp.exp(jnp.sum(lq_ref[...] * lk_ref[...], axis=1, keepdims=True))
    return e[0:1] - e[1:2] + lam_init


def _diff_finish(acc1, l1, acc2, l2, lam, lam_init, subln, z):
    o = acc1 / l1 - lam * (acc2 / l2)
    o = _rms_rows(o) * subln * (1.0 - lam_init)
    return o * _silu(z)


def _flash_kernel(q_ref, k_ref, v_ref, z_ref, lq_ref, lk_ref, sub_ref, o_ref,
                  m_sc, l_sc, acc_sc, *, lam_init):
    qi = pl.program_id(2)
    ki = pl.program_id(3)
    blk = q_ref.shape[0]

    @pl.when(ki == 0)
    def _():
        m_sc[...] = jnp.full(m_sc.shape, NEG_INF, F32)
        l_sc[...] = jnp.zeros(l_sc.shape, F32)
        acc_sc[...] = jnp.zeros(acc_sc.shape, F32)

    @pl.when(ki <= qi)
    def _():
        q = q_ref[...]
        k = k_ref[...]
        v = v_ref[...]
        lane = lax.broadcasted_iota(jnp.int32, (1, LANES), 1)
        row = lax.broadcasted_iota(jnp.int32, (blk, blk), 0)
        col = lax.broadcasted_iota(jnp.int32, (blk, blk), 1)
        visible = (row + (qi - ki) * blk) >= col
        for c in range(2):
            in_map = (lane >= c * ATT_DH) & (lane < (c + 1) * ATT_DH)
            qc = jnp.where(in_map, q, jnp.zeros_like(q))
            s = lax.dot_general(qc, k, (((1,), (1,)), ((), ())), preferred_element_type=F32)
            s = jnp.where(visible, s, NEG_INF)
            m_old = m_sc[c]
            m_new = jnp.maximum(m_old, jnp.max(s, axis=1, keepdims=True))
            alpha = jnp.exp(m_old - m_new)
            p = jnp.exp(s - m_new)
            l_sc[c] = alpha * l_sc[c] + jnp.sum(p, axis=1, keepdims=True)
            acc_sc[c] = alpha * acc_sc[c] + jnp.dot(p.astype(BF16), v, preferred_element_type=F32)
            m_sc[c] = m_new

    @pl.when(ki == qi)
    def _():
        lam = _diff_lambda_in_kernel(lq_ref, lk_ref, lam_init)
        o = _diff_finish(acc_sc[0], l_sc[0], acc_sc[1], l_sc[1], lam, lam_init, sub_ref[...], z_ref[...])
        o_ref[...] = o.astype(o_ref.dtype)


def _flash_attention(qb, kb, vb, z, lq, lk, subln, lam_init, batch, seq):
    n = batch * seq
    nblk = seq // ATT_BLOCK
    q_map = lambda b, h, qi, ki: (b * nblk + qi, h)
    kv_map = lambda b, h, qi, ki: (b * nblk + jnp.minimum(ki, qi), h)
    fixed = lambda b, h, qi, ki: (0, 0)
    return pl.pallas_call(
        functools.partial(_flash_kernel, lam_init=lam_init),
        grid=(batch, ATT_HEADS, nblk, nblk),
        in_specs=[pl.BlockSpec((ATT_BLOCK, LANES), q_map),
                  pl.BlockSpec((ATT_BLOCK, LANES), kv_map),
                  pl.BlockSpec((ATT_BLOCK, LANES), kv_map),
                  pl.BlockSpec((ATT_BLOCK, LANES), q_map),
                  pl.BlockSpec((2, ATT_DH), fixed),
                  pl.BlockSpec((2, ATT_DH), fixed),
                  pl.BlockSpec((1, LANES), fixed)],
        out_specs=pl.BlockSpec((ATT_BLOCK, LANES), q_map),
        out_shape=jax.ShapeDtypeStruct((n, ATT_HEADS * ATT_VD), BF16),
        scratch_shapes=[pltpu.VMEM((2, ATT_BLOCK, 1), F32),
                        pltpu.VMEM((2, ATT_BLOCK, 1), F32),
                        pltpu.VMEM((2, ATT_BLOCK, LANES), F32)],
        compiler_params=_params(4),
        name="flash_diff_attention",
    )(qb, kb, vb, z, lq, lk, subln.reshape(1, LANES))


def _sample_attn_kernel(pt_ref, q_ref, kc_ref, vc_ref, kn_ref, vn_ref, z_ref, lq_ref, lk_ref, sub_ref,
                        o_ref, m_sc, l_sc, acc_sc, *, lam_init, n_pages):
    del pt_ref
    p = pl.program_id(1)
    n_q = q_ref.shape[1]

    @pl.when(p == 0)
    def _():
        m_sc[...] = jnp.full(m_sc.shape, NEG_INF, F32)
        l_sc[...] = jnp.zeros(l_sc.shape, F32)
        acc_sc[...] = jnp.zeros(acc_sc.shape, F32)

    def update(k_ref, v_ref, causal):
        n_kv = v_ref.shape[0]
        v = v_ref[...].astype(BF16)
        row = lax.broadcasted_iota(jnp.int32, (n_q, n_kv), 0)
        col = lax.broadcasted_iota(jnp.int32, (n_q, n_kv), 1)
        visible = (row >> 3) == (col & (ATT_HEADS - 1))
        if causal:
            visible = visible & ((col >> 3) <= (row & 7))
        for c in range(2):
            kc = k_ref[pl.ds(c, n_kv, stride=2), :].astype(BF16)
            s = lax.dot_general(q_ref[c].astype(BF16), kc, (((1,), (1,)), ((), ())),
                                preferred_element_type=F32)
            s = jnp.where(visible, s, NEG_INF)
            m_old = m_sc[c]
            m_new = jnp.maximum(m_old, jnp.max(s, axis=1, keepdims=True))
            alpha = jnp.exp(m_old - m_new)
            e = jnp.exp(s - m_new)
            l_sc[c] = alpha * l_sc[c] + jnp.sum(e, axis=1, keepdims=True)
            acc_sc[c] = alpha * acc_sc[c] + jnp.dot(e.astype(BF16), v, preferred_element_type=F32)
            m_sc[c] = m_new

    @pl.when(p < n_pages)
    def _():
        update(kc_ref, vc_ref, False)

    @pl.when(p == n_pages)
    def _():
        update(kn_ref, vn_ref, True)
        lam = _diff_lambda_in_kernel(lq_ref, lk_ref, lam_init)
        o_ref[...] = _diff_finish(acc_sc[0], l_sc[0], acc_sc[1], l_sc[1], lam, lam_init,
                                  sub_ref[...], z_ref[...])


def _sample_attention(q, k_new, v_new, z, cache_k, cache_v, layer, page_table, lq, lk, subln, lam_init):
    batch, n_pages = page_table.shape
    t = q.shape[0] // batch
    h, dh, vd = ATT_HEADS, ATT_DH, ATT_VD
    n_pool = cache_k.shape[1]
    ck = cache_k.reshape(cache_k.shape[0], n_pool, PAGE_SIZE * h * 2, dh)
    cv = cache_v.reshape(cache_v.shape[0], n_pool, PAGE_SIZE * h, vd)
    q4 = q.reshape(batch, t, h, 2, dh).transpose(0, 3, 2, 1, 4).reshape(batch, 2, h * t, dh)
    z3 = z.reshape(batch, t, h, vd).transpose(0, 2, 1, 3).reshape(batch, h * t, vd)
    kn = k_new.reshape(batch * t * h * 2, dh)
    vn = v_new.reshape(batch * t * h, vd)
    pt = page_table.reshape(-1)

    def page(b, p, pt_ref):
        return (layer, pt_ref[b * n_pages + jnp.minimum(p, n_pages - 1)], 0, 0)

    per_b3 = lambda b, p, pt_ref: (b, 0, 0)
    per_b2 = lambda b, p, pt_ref: (b, 0)
    fixed = lambda b, p, pt_ref: (0, 0)
    grid_spec = pltpu.PrefetchScalarGridSpec(
        num_scalar_prefetch=1,
        grid=(batch, n_pages + 1),
        in_specs=[pl.BlockSpec((None, 2, h * t, dh), lambda b, p, pt_ref: (b, 0, 0, 0)),
                  pl.BlockSpec((None, None, PAGE_SIZE * h * 2, dh), page),
                  pl.BlockSpec((None, None, PAGE_SIZE * h, vd), page),
                  pl.BlockSpec((t * h * 2, dh), per_b2),
                  pl.BlockSpec((t * h, vd), per_b2),
                  pl.BlockSpec((None, h * t, vd), per_b3),
                  pl.BlockSpec((2, dh), fixed),
                  pl.BlockSpec((2, dh), fixed),
                  pl.BlockSpec((1, vd), fixed)],
        out_specs=pl.BlockSpec((None, h * t, vd), per_b3),
        scratch_shapes=[pltpu.VMEM((2, h * t, 1), F32),
                        pltpu.VMEM((2, h * t, 1), F32),
                        pltpu.VMEM((2, h * t, vd), F32)],
    )
    o = pl.pallas_call(
        functools.partial(_sample_attn_kernel, lam_init=lam_init, n_pages=n_pages),
        grid_spec=grid_spec,
        out_shape=jax.ShapeDtypeStruct((batch, h * t, vd), F32),
        compiler_params=_params(2),
        name="paged_diff_attention",
    )(pt, q4, ck, cv, kn, vn, z3, lq, lk, subln.reshape(1, vd))
    return o.reshape(batch, h, t, vd).transpose(0, 2, 1, 3).reshape(batch * t, h * vd)


def _out_proj_kernel(g_ref, w_ref, x_ref, y_ref):
    y_ref[...] = x_ref[...] + jnp.dot(g_ref[...].astype(BF16), w_ref[...], preferred_element_type=F32)


def _out_proj(g, w_bf16, x):
    n = x.shape[0]
    row = lambda i: (i, 0)
    return pl.pallas_call(
        _out_proj_kernel,
        grid=(n // ROW_TILE,),
        in_specs=[pl.BlockSpec((ROW_TILE, g.shape[1]), row),
                  pl.BlockSpec(w_bf16.shape, lambda i: (0, 0)),
                  pl.BlockSpec((ROW_TILE, D_MODEL), row)],
        out_specs=pl.BlockSpec((ROW_TILE, D_MODEL), row),
        out_shape=jax.ShapeDtypeStruct((n, D_MODEL), F32),
        compiler_params=_params(1),
        name="out_proj",
    )(g, w_bf16, x)


def _conformer_kernel(*refs, has_state):
    if has_state:
        (a_ref, b_ref, z_ref, x_ref, st_ref, dw_ref, db_ref, lg_ref, lb_ref, w_ref,
         y_ref, nb_ref, ubuf) = refs
    else:
        (a_ref, b_ref, z_ref, x_ref, dw_ref, db_ref, lg_ref, lb_ref, w_ref,
         y_ref, nb_ref, ubuf) = refs
    ti = pl.program_id(1)
    nseq, tt, ch = a_ref.shape
    past = CONV_WIDTH - 1
    lead = CONV_HALO - past

    @pl.when(ti == 0)
    def _():
        ubuf[:, 0:CONV_HALO, :] = jnp.zeros((nseq, CONV_HALO, ch), F32)
        if has_state:
            ubuf[:, lead:CONV_HALO, :] = st_ref[...]

    @pl.when(ti > 0)
    def _():
        ubuf[:, 0:CONV_HALO, :] = ubuf[:, tt:tt + CONV_HALO, :]

    ubuf[:, CONV_HALO:CONV_HALO + tt, :] = a_ref[...] * _sigmoid(b_ref[...])
    acc = jnp.zeros((nseq, tt, ch), F32) + db_ref[...]
    for j in range(CONV_WIDTH):
        acc = acc + dw_ref[j] * ubuf[:, lead + j:lead + j + tt, :]
    mean = jnp.mean(acc, axis=-1, keepdims=True)
    cen = acc - mean
    var = jnp.mean(cen * cen, axis=-1, keepdims=True)
    c = _silu(cen * lax.rsqrt(var + EPS) * lg_ref[...] + lb_ref[...])
    gated = (c * _silu(z_ref[...])).reshape(nseq * tt, ch).astype(BF16)
    y = jnp.dot(gated, w_ref[...], preferred_element_type=F32)
    y_ref[...] = x_ref[...] + y.reshape(nseq, tt, ch)

    @pl.when(ti == pl.num_programs(1) - 1)
    def _():
        nb_ref[...] = ubuf[:, tt + lead:tt + CONV_HALO, :]


def _conformer(proj, x, state, dw_w, dw_b, ln_g, ln_b, w_out_bf16, nseq, tt):
    bsz, t, _ = x.shape
    ch = D_MODEL
    past = CONV_WIDTH - 1
    col = lambda j: (lambda bi, ti: (bi, ti, j))
    fixed2 = lambda bi, ti: (0, 0)
    in_specs = [pl.BlockSpec((nseq, tt, ch), col(0)),
                pl.BlockSpec((nseq, tt, ch), col(1)),
                pl.BlockSpec((nseq, tt, ch), col(2)),
                pl.BlockSpec((nseq, tt, ch), col(0))]
    args = [proj, proj, proj, x]
    if state is not None:
        in_specs.append(pl.BlockSpec((nseq, past, ch), lambda bi, ti: (bi, 0, 0)))
        args.append(state)
    in_specs += [pl.BlockSpec((CONV_WIDTH, 1, ch), lambda bi, ti: (0, 0, 0)),
                 pl.BlockSpec((1, ch), fixed2),
                 pl.BlockSpec((1, ch), fixed2),
                 pl.BlockSpec((1, ch), fixed2),
                 pl.BlockSpec((ch, D_MODEL), fixed2)]
    args += [dw_w.reshape(CONV_WIDTH, 1, ch), dw_b.reshape(1, ch), ln_g.reshape(1, ch), ln_b.reshape(1, ch),
             w_out_bf16]
    return pl.pallas_call(
        functools.partial(_conformer_kernel, has_state=state is not None),
        grid=(bsz // nseq, t // tt),
        in_specs=in_specs,
        out_specs=[pl.BlockSpec((nseq, tt, D_MODEL), col(0)),
                   pl.BlockSpec((nseq, past, ch), lambda bi, ti: (bi, 0, 0))],
        out_shape=[jax.ShapeDtypeStruct((bsz, t, D_MODEL), F32),
                   jax.ShapeDtypeStruct((bsz, past, ch), F32)],
        scratch_shapes=[pltpu.VMEM((nseq, CONV_HALO + tt, ch), F32)],
        compiler_params=_params(2),
        name="conformer_conv",
    )(*args)


def _unit_lower_inverse(m, eye, n_stage):
    nk = -m
    inv = eye + nk
    for _ in range(n_stage - 1):
        nk = _mm(nk, nk)
        inv = inv + _mm(inv, nk)
    resid = eye - inv - _mm_split(m, inv)
    return inv + _mm(inv, resid)


def _delta_kernel(*refs, seq_len, has_state):
    if has_state:
        (x_ref, z_ref, ab_ref, abt_ref, cst_ref, s0_ref, cw_ref, hp_ref, on_ref,
         g_ref, nc_ref, s_ref, cbuf) = refs
    else:
        (x_ref, z_ref, ab_ref, abt_ref, cw_ref, hp_ref, on_ref,
         g_ref, nc_ref, s_ref, cbuf) = refs
    ci = pl.program_id(1)
    rows, width = x_ref.shape
    nseq = rows // seq_len
    past = DN_CONV - 1
    lead = DN_HALO - past
    dk = DN_DK

    @pl.when(ci == 0)
    def _():
        cbuf[:, 0:DN_HALO, :] = jnp.zeros((nseq, DN_HALO, width), F32)
        if has_state:
            cbuf[:, lead:DN_HALO, :] = cst_ref[...]
            s_ref[...] = s0_ref[...]
        else:
            s_ref[...] = jnp.zeros(s_ref.shape, F32)

    @pl.when(ci > 0)
    def _():
        cbuf[:, 0:DN_HALO, :] = cbuf[:, seq_len:seq_len + DN_HALO, :]

    cbuf[:, DN_HALO:DN_HALO + seq_len, :] = x_ref[...].reshape(nseq, seq_len, width)
    conv = jnp.zeros((nseq, seq_len, width), F32)
    for j in range(DN_CONV):
        conv = conv + cw_ref[j] * cbuf[:, lead + j:lead + j + seq_len, :]
    qkv = _silu(conv).reshape(rows, width)

    @pl.when(ci == pl.num_programs(1) - 1)
    def _():
        nc_ref[...] = cbuf[:, seq_len + lead:seq_len + DN_HALO, :]

    ri = lax.broadcasted_iota(jnp.int32, (rows, rows), 0)
    cj = lax.broadcasted_iota(jnp.int32, (rows, rows), 1)
    shift = int(math.log2(seq_len))
    same = (ri >> shift) == (cj >> shift)
    lower = (ri >= cj) & same
    strict = (ri > cj) & same
    upper = (ri <= cj) & same
    eye = (ri == cj).astype(F32)
    row_seq = lax.broadcasted_iota(jnp.int32, (rows, 1), 0) >> shift
    ab = ab_ref[...]
    abt = abt_ref[...]
    hp = hp_ref[...]

    for h in range(DN_HEADS):
        q = qkv[:, h * dk:(h + 1) * dk]
        k = qkv[:, (DN_HEADS + h) * dk:(DN_HEADS + h + 1) * dk]
        v = qkv[:, (2 * DN_HEADS + h) * dk:(2 * DN_HEADS + h + 1) * dk]
        q = q * lax.rsqrt(jnp.sum(q * q, axis=-1, keepdims=True) + EPS) * (dk ** -0.5)
        k = k * lax.rsqrt(jnp.sum(k * k, axis=-1, keepdims=True) + EPS)
        neg_a = -jnp.exp(hp[0:1, h:h + 1])
        dt_bias = hp[1:2, h:h + 1]
        g_col = neg_a * _softplus(ab[:, h:h + 1] + dt_bias)
        g_row = neg_a * _softplus(abt[h:h + 1, :] + dt_bias)
        beta = _sigmoid(ab[:, DN_HEADS + h:DN_HEADS + h + 1])
        gc_col = jnp.sum(jnp.where(lower, g_row, 0.0), axis=1, keepdims=True)
        gc_row = jnp.sum(jnp.where(upper, g_col, 0.0), axis=0, keepdims=True)
        g_tot = jnp.sum(jnp.where(same, g_row, 0.0), axis=1, keepdims=True)
        decay = jnp.where(lower, jnp.exp(jnp.where(lower, gc_col - gc_row, 0.0)), 0.0)
        kb = k * beta
        vb = v * beta
        m = jnp.where(strict, _mm_nt(kb, k) * decay, 0.0)
        tinv = _unit_lower_inverse(m, eye, shift)
        u = _mm(tinv, vb)
        w = _mm(tinv, kb * jnp.exp(gc_col))
        qk = _mm_nt(q, k) * decay
        qg = q * jnp.exp(gc_col)
        kd = k * jnp.exp(g_tot - gc_col)
        e_tot = jnp.exp(g_tot)

        if nseq == 1:
            s_old = s_ref[0, h]
            v_new = u - _mm(w, s_old)
            o = _mm(qg, s_old) + _mm(qk, v_new)
            s_ref[0, h] = s_old * e_tot[0:1, :] + _mm_tn(kd, v_new)
        else:
            def read_state(s, carry):
                ws, qs = carry
                mine = row_seq == s
                s_old = s_ref[s, h]
                ws = ws + _mm(jnp.where(mine, w, 0.0), s_old)
                qs = qs + _mm(jnp.where(mine, qg, 0.0), s_old)
                return ws, qs

            zero = jnp.zeros((rows, dk), F32)
            ws, qs = lax.fori_loop(0, nseq, read_state, (zero, zero))
            v_new = u - ws
            o = qs + _mm(qk, v_new)

            def write_state(s, carry):
                mine = row_seq == s
                e_s = jnp.max(jnp.where(mine, e_tot, 0.0), axis=0, keepdims=True)
                s_ref[s, h] = s_ref[s, h] * e_s + _mm_tn(jnp.where(mine, kd, 0.0), v_new)
                return carry

            lax.fori_loop(0, nseq, write_state, 0)

        o = _rms_rows(o) * on_ref[...]
        g_ref[:, h * dk:(h + 1) * dk] = (o * _silu(z_ref[:, h * dk:(h + 1) * dk])).astype(g_ref.dtype)


def _gated_delta(proj, conv_state, s0, conv_w, a_log, dt_bias, o_norm, bsz, t, seq_len):
    n = bsz * t
    rows = DN_ROWS
    nseq = rows // seq_len
    n_chunk = max(t // rows, 1)
    n_grp = n // (rows * n_chunk)
    past = DN_CONV - 1
    has_state = conv_state is not None
    ab = proj[:, DN_QKV + D_MODEL:DN_QKV + D_MODEL + 2 * DN_HEADS]
    abt = ab.reshape(n // rows, rows, 2 * DN_HEADS).transpose(0, 2, 1)
    hp = jnp.zeros((2, LANES), F32).at[0, :DN_HEADS].set(a_log).at[1, :DN_HEADS].set(dt_bias)
    blk = lambda j: (lambda gi, ci: (gi * n_chunk + ci, j))
    per_grp3 = lambda gi, ci: (gi, 0, 0)
    per_grp4 = lambda gi, ci: (gi, 0, 0, 0)
    fixed2 = lambda gi, ci: (0, 0)
    in_specs = [pl.BlockSpec((rows, DN_QKV), blk(0)),
                pl.BlockSpec((rows, D_MODEL), blk(DN_QKV // D_MODEL)),
                pl.BlockSpec((rows, LANES), blk((DN_QKV + D_MODEL) // LANES)),
                pl.BlockSpec((None, 2 * DN_HEADS, rows), lambda gi, ci: (gi * n_chunk + ci, 0, 0))]
    args = [proj, proj, proj, abt]
    if has_state:
        in_specs += [pl.BlockSpec((nseq, past, DN_QKV), per_grp3),
                     pl.BlockSpec((nseq, DN_HEADS, DN_DK, DN_DK), per_grp4)]
        args += [conv_state, s0]
    in_specs += [pl.BlockSpec((DN_CONV, 1, DN_QKV), lambda gi, ci: (0, 0, 0)),
                 pl.BlockSpec((2, LANES), fixed2),
                 pl.BlockSpec((1, LANES), fixed2)]
    args += [conv_w.reshape(DN_CONV, 1, DN_QKV), hp, o_norm.reshape(1, LANES)]
    n_state = n_grp * nseq
    return pl.pallas_call(
        functools.partial(_delta_kernel, seq_len=seq_len, has_state=has_state),
        grid=(n_grp, n_chunk),
        in_specs=in_specs,
        out_specs=[pl.BlockSpec((rows, D_MODEL), blk(0)),
                   pl.BlockSpec((nseq, past, DN_QKV), per_grp3),
                   pl.BlockSpec((nseq, DN_HEADS, DN_DK, DN_DK), per_grp4)],
        out_shape=[jax.ShapeDtypeStruct((n, D_MODEL), BF16),
                   jax.ShapeDtypeStruct((n_state, past, DN_QKV), F32),
                   jax.ShapeDtypeStruct((n_state, DN_HEADS, DN_DK, DN_DK), F32)],
        scratch_shapes=[pltpu.VMEM((nseq, DN_HALO + seq_len, DN_QKV), F32)],
        compiler_params=_params(2),
        name="gated_delta",
    )(*args)


def kernel(x_prompt, x_sample, cache_k, cache_v, page_table, state_conv, state_delta_conv, state_delta_S, norm_gain, attn_w_in, attn_q_norm, attn_k_norm, attn_lambda_q, attn_lambda_k, attn_subln, attn_w_out, conv_w_in, conv_dw_w, conv_dw_b, conv_ln_g, conv_ln_b, conv_w_out, delta_w_in, delta_conv_w, delta_a_log, delta_dt_bias, delta_o_norm, delta_w_out):
    bp, tp, _ = x_prompt.shape
    bs, ts, _ = x_sample.shape
    past_len = page_table.shape[1] * PAGE_SIZE
    yp = x_prompt.reshape(bp * tp, D_MODEL)
    ys = x_sample.reshape(bs * ts, D_MODEL)

    cos_p, sin_p = _rope_tables(jnp.arange(tp))
    cos_s, sin_s = _rope_tables(past_len + jnp.arange(ts))
    cos_s = jnp.tile(cos_s, (ROW_TILE // ts, 1))
    sin_s = jnp.tile(sin_s, (ROW_TILE // ts, 1))

    kp_l, vp_l, ks_l, vs_l = [], [], [], []
    cp_l, cs_l = [], []
    dcp_l, dcs_l, dsp_l, dss_l = [], [], [], []
    for i in range(DEPTH):
        kind = i % N_MIXERS
        j = i // N_MIXERS
        gain = norm_gain[i]
        if kind == 0:
            lam_init = 0.8 - 0.6 * math.exp(-0.3 * i)
            w_in = attn_w_in[j].astype(BF16)
            w_out = attn_w_out[j].astype(BF16)
            lq, lk = attn_lambda_q[j], attn_lambda_k[j]
            qb, k, v, z, kb, vb = _attn_proj(yp, gain, w_in, attn_q_norm[j], attn_k_norm[j], cos_p, sin_p, BF16)
            gp = _flash_attention(qb, kb, vb, z, lq, lk, attn_subln[j], lam_init, bp, tp)
            kp_l.append(k.reshape(bp, tp, ATT_HEADS, 2, ATT_DH))
            vp_l.append(v.reshape(bp, tp, ATT_HEADS, ATT_VD))
            yp = _out_proj(gp, w_out, yp)
            q, k, v, z, _, _ = _attn_proj(ys, gain, w_in, attn_q_norm[j], attn_k_norm[j], cos_s, sin_s, F32)
            gs = _sample_attention(q, k, v, z, cache_k, cache_v, j, page_table, lq, lk, attn_subln[j], lam_init)
            ks_l.append(k.reshape(bs, ts, ATT_HEADS, 2, ATT_DH))
            vs_l.append(v.reshape(bs, ts, ATT_HEADS, ATT_VD))
            ys = _out_proj(gs, w_out, ys)
        elif kind == 1:
            w_in = conv_w_in[j].astype(BF16)
            w_out = conv_w_out[j].astype(BF16)
            tail = (conv_dw_w[j], conv_dw_b[j], conv_ln_g[j], conv_ln_b[j], w_out)
            proj = _norm_proj(yp, gain, w_in).reshape(bp, tp, 3 * D_MODEL)
            y3, buf = _conformer(proj, yp.reshape(bp, tp, D_MODEL), None, *tail, nseq=1, tt=CONV_TILE)
            yp = y3.reshape(bp * tp, D_MODEL)
            cp_l.append(buf)
            proj = _norm_proj(ys, gain, w_in).reshape(bs, ts, 3 * D_MODEL)
            y3, buf = _conformer(proj, ys.reshape(bs, ts, D_MODEL), state_conv[j], *tail,
                                 nseq=DN_ROWS // ts, tt=ts)
            ys = y3.reshape(bs * ts, D_MODEL)
            cs_l.append(buf)
        else:
            n_in = delta_w_in.shape[2]
            n_pad = -n_in % LANES
            w_in = jnp.pad(delta_w_in[j], ((0, 0), (0, n_pad))).astype(BF16)
            w_out = delta_w_out[j].astype(BF16)
            tail = (delta_conv_w[j], delta_a_log[j], delta_dt_bias[j], delta_o_norm[j])
            proj = _norm_proj(yp, gain, w_in)
            g, buf, s_new = _gated_delta(proj, None, None, *tail, bsz=bp, t=tp, seq_len=DN_ROWS)
            yp = _out_proj(g, w_out, yp)
            dcp_l.append(buf)
            dsp_l.append(s_new)
            proj = _norm_proj(ys, gain, w_in)
            g, buf, s_new = _gated_delta(proj, state_delta_conv[j], state_delta_S[j], *tail,
                                         bsz=bs, t=ts, seq_len=ts)
            ys = _out_proj(g, w_out, ys)
            dcs_l.append(buf)
            dss_l.append(s_new)
    return (yp.reshape(bp, tp, D_MODEL), ys.reshape(bs, ts, D_MODEL),
            jnp.stack(kp_l), jnp.stack(vp_l), jnp.stack(ks_l), jnp.stack(vs_l),
            jnp.stack(cp_l), jnp.stack(cs_l), jnp.stack(dcp_l), jnp.stack(dcs_l),
            jnp.stack(dsp_l), jnp.stack(dss_l))
```

```python
import functools
import math

import jax
import jax.numpy as jnp
from jax import lax
from jax.experimental import pallas as pl
from jax.experimental.pallas import tpu as pltpu

F32 = jnp.float32
BF16 = jnp.bfloat16

D_MODEL = 1024
DEPTH = 4
N_MIXERS = 3
PAGE_SIZE = 128
ATT_HEADS = 8
ATT_DH = 64
ATT_VD = 128
ROPE_THETA = 10000.0
NEG_INF = -1e30
CONV_WIDTH = 31
DN_HEADS = 8
DN_DK = 128
DN_QKV = 3072
DN_CONV = 4
EPS = 1e-6

LANES = 128
ROW_TILE = 256
ATT_BLOCK = 512
PAGES_PER_STEP = 4
CONV_TILE = 256
CONV_HALO = 32
DN_ROWS = 128
DN_HALO = 8
VMEM_LIMIT = 56 * 1024 * 1024


def _params(n_axes):
    return pltpu.CompilerParams(dimension_semantics=("arbitrary",) * n_axes,
                                vmem_limit_bytes=VMEM_LIMIT)


def _mm(a, b):
    return jnp.dot(a.astype(BF16), b.astype(BF16), preferred_element_type=F32)


def _mm_nt(a, b):
    return lax.dot_general(a.astype(BF16), b.astype(BF16), (((1,), (1,)), ((), ())),
                           preferred_element_type=F32)


def _mm_tn(a, b):
    return lax.dot_general(a.astype(BF16), b.astype(BF16), (((0,), (0,)), ((), ())),
                           preferred_element_type=F32)


def _mm_split(a, b):
    ah = a.astype(BF16)
    al = (a - ah.astype(F32)).astype(BF16)
    bh = b.astype(BF16)
    bl = (b - bh.astype(F32)).astype(BF16)
    dot = functools.partial(jnp.dot, preferred_element_type=F32)
    return dot(ah, bh) + dot(ah, bl) + dot(al, bh)


def _sigmoid(x):
    return 1.0 / (1.0 + jnp.exp(-x))


def _silu(x):
    return x * _sigmoid(x)


def _softplus(x):
    return jnp.maximum(x, 0.0) + jnp.log(1.0 + jnp.exp(-jnp.abs(x)))


def _rms_rows(x):
    return x * lax.rsqrt(jnp.mean(x * x, axis=-1, keepdims=True) + EPS)


def _norm_proj_kernel(x_ref, g_ref, w_ref, o_ref, *, col_chunk):
    h = (_rms_rows(x_ref[...]) * g_ref[...]).astype(BF16)
    n_out = o_ref.shape[1]
    for c0 in range(0, n_out, col_chunk):
        c1 = min(c0 + col_chunk, n_out)
        o_ref[:, c0:c1] = jnp.dot(h, w_ref[:, c0:c1], preferred_element_type=F32)


def _norm_proj(x, gain, w_bf16):
    n, n_out = x.shape[0], w_bf16.shape[1]
    return pl.pallas_call(
        functools.partial(_norm_proj_kernel, col_chunk=1024),
        grid=(n // ROW_TILE,),
        in_specs=[pl.BlockSpec((ROW_TILE, D_MODEL), lambda i: (i, 0)),
                  pl.BlockSpec((1, D_MODEL), lambda i: (0, 0)),
                  pl.BlockSpec((D_MODEL, n_out), lambda i: (0, 0))],
        out_specs=pl.BlockSpec((ROW_TILE, n_out), lambda i: (i, 0)),
        out_shape=jax.ShapeDtypeStruct((n, n_out), F32),
        compiler_params=_params(1),
        name="norm_proj",
    )(x, gain.reshape(1, D_MODEL), w_bf16)


def _attn_proj_kernel(x_ref, g_ref, w_ref, qn_ref, kn_ref, cos_ref, sin_ref,
                      q_ref, k_ref, v_ref, z_ref, kb_ref, vb_ref):
    h = (_rms_rows(x_ref[...]) * g_ref[...]).astype(BF16)
    width = ATT_HEADS * ATT_VD
    r = lax.broadcasted_iota(jnp.int32, (LANES, LANES), 0)
    c = lax.broadcasted_iota(jnp.int32, (LANES, LANES), 1)
    group_ones = ((r >> 6) == (c >> 6)).astype(BF16)
    lane = lax.broadcasted_iota(jnp.int32, (1, LANES), 1)
    first_half = (lane & (ATT_DH - 1)) < (ATT_DH // 2)
    cos = cos_ref[...]
    sin = sin_ref[...]

    def norm_rope(raw, gain):
        ms = jnp.dot((raw * raw).astype(BF16), group_ones, preferred_element_type=F32) * (1.0 / ATT_DH)
        xn = raw * lax.rsqrt(ms + EPS) * gain
        rot = jnp.where(first_half, pltpu.roll(xn, LANES - ATT_DH // 2, 1), pltpu.roll(xn, ATT_DH // 2, 1))
        return xn * cos + rot * sin

    pair = 2 * LANES
    for c0 in range(0, width, pair):
        q_raw = jnp.dot(h, w_ref[:, c0:c0 + pair], preferred_element_type=F32)
        k_raw = jnp.dot(h, w_ref[:, width + c0:width + c0 + pair], preferred_element_type=F32)
        for s in range(0, pair, LANES):
            sl = slice(c0 + s, c0 + s + LANES)
            q = norm_rope(q_raw[:, s:s + LANES], qn_ref[...]) * (ATT_DH ** -0.5)
            q_ref[:, sl] = q.astype(q_ref.dtype)
            k = norm_rope(k_raw[:, s:s + LANES], kn_ref[...])
            k_ref[:, sl] = k
            kb_ref[:, sl] = k.astype(BF16)
    v = jnp.dot(h, w_ref[:, 2 * width:3 * width], preferred_element_type=F32)
    v_ref[...] = v
    vb_ref[...] = v.astype(BF16)
    z_ref[...] = jnp.dot(h, w_ref[:, 3 * width:4 * width], preferred_element_type=F32)


def _attn_proj(x, gain, w_bf16, qn, kn, cos, sin, q_dtype):
    n = x.shape[0]
    n_tab = cos.shape[0] // ROW_TILE
    width = ATT_HEADS * ATT_VD
    row = lambda i: (i, 0)
    fixed = lambda i: (0, 0)
    out = lambda dt: jax.ShapeDtypeStruct((n, width), dt)
    return pl.pallas_call(
        _attn_proj_kernel,
        grid=(n // ROW_TILE,),
        in_specs=[pl.BlockSpec((ROW_TILE, D_MODEL), row),
                  pl.BlockSpec((1, D_MODEL), fixed),
                  pl.BlockSpec((D_MODEL, 4 * width), fixed),
                  pl.BlockSpec((1, LANES), fixed),
                  pl.BlockSpec((1, LANES), fixed),
                  pl.BlockSpec((ROW_TILE, LANES), lambda i: (i % n_tab, 0)),
                  pl.BlockSpec((ROW_TILE, LANES), lambda i: (i % n_tab, 0))],
        out_specs=[pl.BlockSpec((ROW_TILE, width), row)] * 6,
        out_shape=[out(q_dtype), out(F32), out(F32), out(F32), out(BF16), out(BF16)],
        compiler_params=_params(1),
        name="attn_proj",
    )(x, gain.reshape(1, D_MODEL), w_bf16, jnp.tile(qn, 2).reshape(1, LANES), jnp.tile(kn, 2).reshape(1, LANES),
      cos, sin)


def _rope_tables(pos):
    inv = ROPE_THETA ** (-jnp.arange(0, ATT_DH, 2, dtype=F32) / ATT_DH)
    ang = pos.astype(F32)[:, None] * inv[None, :]
    cos = jnp.cos(ang)
    sin = jnp.sin(ang)
    cos = jnp.concatenate([cos, cos, cos, cos], axis=-1)
    sin = jnp.concatenate([-sin, sin, -sin, sin], axis=-1)
    return cos, sin


def _diff_lambda_in_kernel(lq_ref, lk_ref, lam_init):
    e = jnp.exp(jnp.sum(lq_ref[...] * lk_ref[...], axis=1, keepdims=True))
    return e[0:1] - e[1:2] + lam_init


def _diff_finish(acc1, l1, acc2, l2, lam, lam_init, subln, z):
    o = acc1 / l1 - lam * (acc2 / l2)
    o = _rms_rows(o) * subln * (1.0 - lam_init)
    return o * _silu(z)


def _flash_kernel(q_ref, k_ref, v_ref, z_ref, lq_ref, lk_ref, sub_ref, o_ref, *, lam_init):
    blk = ATT_BLOCK
    nblk = q_ref.shape[0] // blk
    lam = _diff_lambda_in_kernel(lq_ref, lk_ref, lam_init)
    lane = lax.broadcasted_iota(jnp.int32, (1, LANES), 1)
    kv_i = lax.broadcasted_iota(jnp.int32, (blk, 2 * blk), 0)
    q_i = lax.broadcasted_iota(jnp.int32, (blk, 2 * blk), 1) & (blk - 1)
    diag_visible = q_i >= kv_i
    for qi in range(nblk):
        rows = slice(qi * blk, (qi + 1) * blk)
        q = q_ref[rows, :]
        zero = jnp.zeros_like(q)
        q_maps = jnp.concatenate([jnp.where(lane < ATT_DH, q, zero), jnp.where(lane >= ATT_DH, q, zero)], axis=0)
        m = jnp.full((1, 2 * blk), NEG_INF, F32)
        l = jnp.zeros((1, 2 * blk), F32)
        acc = jnp.zeros((ATT_VD, 2 * blk), F32)
        for ki in range(qi + 1):
            kv_rows = slice(ki * blk, (ki + 1) * blk)
            s = lax.dot_general(k_ref[kv_rows, :], q_maps, (((1,), (1,)), ((), ())), preferred_element_type=F32)
            if ki == qi:
                s = jnp.where(diag_visible, s, NEG_INF)
            m_new = jnp.maximum(m, jnp.max(s, axis=0, keepdims=True))
            alpha = jnp.exp(m - m_new)
            p = jnp.exp(s - m_new)
            l = alpha * l + jnp.sum(p, axis=0, keepdims=True)
            pv = lax.dot_general(v_ref[kv_rows, :], p.astype(BF16), (((0,), (0,)), ((), ())),
                                 preferred_element_type=F32)
            acc = alpha * acc + pv
            m = m_new
        o_t = acc[:, :blk] / l[:, :blk] - lam * (acc[:, blk:] / l[:, blk:])
        o_t = o_t * lax.rsqrt(jnp.mean(o_t * o_t, axis=0, keepdims=True) + EPS) * sub_ref[...] * (1.0 - lam_init)
        o_ref[rows, :] = (o_t.T * _silu(z_ref[rows, :])).astype(o_ref.dtype)


def _flash_attention(qb, kb, vb, z, lq, lk, subln, lam_init, batch, seq):
    n = batch * seq
    per_head = lambda b, h: (b, h)
    fixed = lambda b, h: (0, 0)
    return pl.pallas_call(
        functools.partial(_flash_kernel, lam_init=lam_init),
        grid=(batch, ATT_HEADS),
        in_specs=[pl.BlockSpec((seq, LANES), per_head),
                  pl.BlockSpec((seq, LANES), per_head),
                  pl.BlockSpec((seq, LANES), per_head),
                  pl.BlockSpec((seq, LANES), per_head),
                  pl.BlockSpec((2, ATT_DH), fixed),
                  pl.BlockSpec((2, ATT_DH), fixed),
                  pl.BlockSpec((ATT_VD, 1), fixed)],
        out_specs=pl.BlockSpec((seq, LANES), per_head),
        out_shape=jax.ShapeDtypeStruct((n, ATT_HEADS * ATT_VD), BF16),
        compiler_params=_params(2),
        name="flash_diff_attention",
    )(qb, kb, vb, z, lq, lk, subln.reshape(ATT_VD, 1))


def _sample_attn_kernel(*refs, lam_init, n_steps):
    g = PAGES_PER_STEP
    q_ref = refs[1]
    kt_refs = refs[2:2 + g]
    v_refs = refs[2 + g:2 + 2 * g]
    kn_ref, vn_ref, z_ref, lq_ref, lk_ref, sub_ref, o_ref, qbd_sc, m_sc, l_sc, acc_sc = refs[2 + 2 * g:]
    p = pl.program_id(1)
    t = q_ref.shape[0]
    width = q_ref.shape[1]
    n_rows = 2 * ATT_HEADS * t
    per_head = 2 * t

    @pl.when(p == 0)
    def _():
        q = jnp.concatenate([q_ref[...]] * (2 * ATT_HEADS), axis=0)
        r = lax.broadcasted_iota(jnp.int32, (n_rows, width), 0)
        c = lax.broadcasted_iota(jnp.int32, (n_rows, width), 1)
        qbd_sc[...] = jnp.where((r >> 3) == (c >> 6), q, 0.0).astype(BF16)
        m_sc[...] = jnp.full(m_sc.shape, NEG_INF, F32)
        l_sc[...] = jnp.zeros(l_sc.shape, F32)
        acc_sc[...] = jnp.zeros(acc_sc.shape, F32)

    def update(s, v_of_head):
        m_old = m_sc[...]
        m_new = jnp.maximum(m_old, jnp.max(s, axis=1, keepdims=True))
        alpha = jnp.exp(m_old - m_new)
        e = jnp.exp(s - m_new)
        l_sc[...] = alpha * l_sc[...] + jnp.sum(e, axis=1, keepdims=True)
        m_sc[...] = m_new
        e = e.astype(BF16)
        for h in range(ATT_HEADS):
            rows = slice(h * per_head, (h + 1) * per_head)
            acc_sc[rows, :] = alpha[rows] * acc_sc[rows, :] + jnp.dot(e[rows], v_of_head(h),
                                                                      preferred_element_type=F32)

    kt = jnp.concatenate([r[...].astype(BF16) for r in kt_refs], axis=1)

    def cached_v(h):
        return jnp.concatenate([r[pl.ds(h, PAGE_SIZE, stride=ATT_HEADS), :].astype(BF16) for r in v_refs], axis=0)

    update(jnp.dot(qbd_sc[...], kt, preferred_element_type=F32), cached_v)

    @pl.when(p == n_steps - 1)
    def _():
        pad = n_rows - t
        kn = jnp.concatenate([kn_ref[...], jnp.zeros((pad, width), F32)], axis=0).astype(BF16)
        s = lax.dot_general(qbd_sc[...], kn, (((1,), (1,)), ((), ())), preferred_element_type=F32)
        row = lax.broadcasted_iota(jnp.int32, (n_rows, n_rows), 0)
        col = lax.broadcasted_iota(jnp.int32, (n_rows, n_rows), 1)
        s = jnp.where(col <= (row & (t - 1)), s, NEG_INF)

        def new_v(h):
            vh = vn_ref[:, h * ATT_VD:(h + 1) * ATT_VD]
            return jnp.concatenate([vh, jnp.zeros((pad, ATT_VD), F32)], axis=0).astype(BF16)

        update(s, new_v)
        lam = _diff_lambda_in_kernel(lq_ref, lk_ref, lam_init)
        acc = acc_sc[...]
        l = l_sc[...]
        for h in range(ATT_HEADS):
            r0 = h * per_head
            cols = slice(h * ATT_VD, (h + 1) * ATT_VD)
            o_ref[:, cols] = _diff_finish(acc[r0:r0 + t], l[r0:r0 + t], acc[r0 + t:r0 + 2 * t],
                                          l[r0 + t:r0 + 2 * t], lam, lam_init, sub_ref[...], z_ref[:, cols])


def _sample_attention(q, k_new, v_new, z, cache_k, cache_v, layer, page_table, lq, lk, subln, lam_init):
    batch, n_pages = page_table.shape
    t = q.shape[0] // batch
    h, dh, vd = ATT_HEADS, ATT_DH, ATT_VD
    width = h * vd
    g = PAGES_PER_STEP
    n_steps = n_pages // g
    assert t == 8 and n_pages % g == 0 and cache_k.shape[2] == PAGE_SIZE
    n_layer, n_pool = cache_k.shape[:2]
    ckt = cache_k.transpose(0, 1, 3, 4, 5, 2).reshape(n_layer, n_pool, width, PAGE_SIZE)
    cv = cache_v.reshape(n_layer, n_pool, PAGE_SIZE * h, vd)
    pt = page_table.reshape(-1)

    def page(i):
        return lambda b, p, pt_ref: (layer, pt_ref[b * n_pages + p * g + i], 0, 0)

    per_b = lambda b, p, pt_ref: (b, 0)
    fixed = lambda b, p, pt_ref: (0, 0)
    grid_spec = pltpu.PrefetchScalarGridSpec(
        num_scalar_prefetch=1,
        grid=(batch, n_steps),
        in_specs=([pl.BlockSpec((t, width), per_b)]
                  + [pl.BlockSpec((None, None, width, PAGE_SIZE), page(i)) for i in range(g)]
                  + [pl.BlockSpec((None, None, PAGE_SIZE * h, vd), page(i)) for i in range(g)]
                  + [pl.BlockSpec((t, width), per_b),
                     pl.BlockSpec((t, width), per_b),
                     pl.BlockSpec((t, width), per_b),
                     pl.BlockSpec((2, dh), fixed),
                     pl.BlockSpec((2, dh), fixed),
                     pl.BlockSpec((1, vd), fixed)]),
        out_specs=pl.BlockSpec((t, width), per_b),
        scratch_shapes=[pltpu.VMEM((2 * h * t, width), BF16),
                        pltpu.VMEM((2 * h * t, 1), F32),
                        pltpu.VMEM((2 * h * t, 1), F32),
                        pltpu.VMEM((2 * h * t, vd), F32)],
    )
    return pl.pallas_call(
        functools.partial(_sample_attn_kernel, lam_init=lam_init, n_steps=n_steps),
        grid_spec=grid_spec,
        out_shape=jax.ShapeDtypeStruct((batch * t, width), F32),
        compiler_params=_params(2),
        name="paged_diff_attention",
    )(pt, q, *([ckt] * g), *([cv] * g), k_new, v_new, z, lq, lk, subln.reshape(1, vd))


def _out_proj_kernel(g_ref, w_ref, x_ref, y_ref):
    y_ref[...] = x_ref[...] + jnp.dot(g_ref[...].astype(BF16), w_ref[...], preferred_element_type=F32)


def _out_proj(g, w_bf16, x):
    n = x.shape[0]
    row = lambda i: (i, 0)
    return pl.pallas_call(
        _out_proj_kernel,
        grid=(n // ROW_TILE,),
        in_specs=[pl.BlockSpec((ROW_TILE, g.shape[1]), row),
                  pl.BlockSpec(w_bf16.shape, lambda i: (0, 0)),
                  pl.BlockSpec((ROW_TILE, D_MODEL), row)],
        out_specs=pl.BlockSpec((ROW_TILE, D_MODEL), row),
        out_shape=jax.ShapeDtypeStruct((n, D_MODEL), F32),
        compiler_params=_params(1),
        name="out_proj",
    )(g, w_bf16, x)


def _conformer_kernel(*refs, has_state):
    if has_state:
        (a_ref, b_ref, z_ref, x_ref, st_ref, dw_ref, db_ref, lg_ref, lb_ref, w_ref,
         y_ref, nb_ref, ubuf) = refs
    else:
        (a_ref, b_ref, z_ref, x_ref, dw_ref, db_ref, lg_ref, lb_ref, w_ref,
         y_ref, nb_ref, ubuf) = refs
    ti = pl.program_id(1)
    nseq, tt, ch = a_ref.shape
    past = CONV_WIDTH - 1
    lead = CONV_HALO - past

    @pl.when(ti == 0)
    def _():
        ubuf[:, 0:CONV_HALO, :] = jnp.zeros((nseq, CONV_HALO, ch), F32)
        if has_state:
            ubuf[:, lead:CONV_HALO, :] = st_ref[...]

    @pl.when(ti > 0)
    def _():
        ubuf[:, 0:CONV_HALO, :] = ubuf[:, tt:tt + CONV_HALO, :]

    ubuf[:, CONV_HALO:CONV_HALO + tt, :] = a_ref[...] * _sigmoid(b_ref[...])
    acc = jnp.zeros((nseq, tt, ch), F32) + db_ref[...]
    for j in range(CONV_WIDTH):
        acc = acc + dw_ref[j] * ubuf[:, lead + j:lead + j + tt, :]
    mean = jnp.mean(acc, axis=-1, keepdims=True)
    cen = acc - mean
    var = jnp.mean(cen * cen, axis=-1, keepdims=True)
    c = _silu(cen * lax.rsqrt(var + EPS) * lg_ref[...] + lb_ref[...])
    gated = (c * _silu(z_ref[...])).reshape(nseq * tt, ch).astype(BF16)
    y = jnp.dot(gated, w_ref[...], preferred_element_type=F32)
    y_ref[...] = x_ref[...] + y.reshape(nseq, tt, ch)

    @pl.when(ti == pl.num_programs(1) - 1)
    def _():
        nb_ref[...] = ubuf[:, tt + lead:tt + CONV_HALO, :]


def _conformer(proj, x, state, dw_w, dw_b, ln_g, ln_b, w_out_bf16, nseq, tt):
    bsz, t, _ = x.shape
    ch = D_MODEL
    past = CONV_WIDTH - 1
    col = lambda j: (lambda bi, ti: (bi, ti, j))
    fixed2 = lambda bi, ti: (0, 0)
    in_specs = [pl.BlockSpec((nseq, tt, ch), col(0)),
                pl.BlockSpec((nseq, tt, ch), col(1)),
                pl.BlockSpec((nseq, tt, ch), col(2)),
                pl.BlockSpec((nseq, tt, ch), col(0))]
    args = [proj, proj, proj, x]
    if state is not None:
        in_specs.append(pl.BlockSpec((nseq, past, ch), lambda bi, ti: (bi, 0, 0)))
        args.append(state)
    in_specs += [pl.BlockSpec((CONV_WIDTH, 1, ch), lambda bi, ti: (0, 0, 0)),
                 pl.BlockSpec((1, ch), fixed2),
                 pl.BlockSpec((1, ch), fixed2),
                 pl.BlockSpec((1, ch), fixed2),
                 pl.BlockSpec((ch, D_MODEL), fixed2)]
    args += [dw_w.reshape(CONV_WIDTH, 1, ch), dw_b.reshape(1, ch), ln_g.reshape(1, ch), ln_b.reshape(1, ch),
             w_out_bf16]
    return pl.pallas_call(
        functools.partial(_conformer_kernel, has_state=state is not None),
        grid=(bsz // nseq, t // tt),
        in_specs=in_specs,
        out_specs=[pl.BlockSpec((nseq, tt, D_MODEL), col(0)),
                   pl.BlockSpec((nseq, past, ch), lambda bi, ti: (bi, 0, 0))],
        out_shape=[jax.ShapeDtypeStruct((bsz, t, D_MODEL), F32),
                   jax.ShapeDtypeStruct((bsz, past, ch), F32)],
        scratch_shapes=[pltpu.VMEM((nseq, CONV_HALO + tt, ch), F32)],
        compiler_params=_params(2),
        name="conformer_conv",
    )(*args)


def _unit_lower_inverse(m, eye, n_stage):
    nk = -m
    inv = eye + nk
    for _ in range(n_stage - 1):
        nk = _mm(nk, nk)
        inv = inv + _mm(inv, nk)
    resid = eye - inv - _mm_split(m, inv)
    return inv + _mm(inv, resid)


def _delta_kernel(*refs, seq_len, has_state):
    if has_state:
        (x_ref, z_ref, ab_ref, abt_ref, cst_ref, s0_ref, cw_ref, hp_ref, on_ref,
         g_ref, nc_ref, s_ref, cbuf) = refs
    else:
        (x_ref, z_ref, ab_ref, abt_ref, cw_ref, hp_ref, on_ref,
         g_ref, nc_ref, s_ref, cbuf) = refs
    ci = pl.program_id(1)
    rows, width = x_ref.shape
    nseq = rows // seq_len
    past = DN_CONV - 1
    lead = DN_HALO - past
    dk = DN_DK

    @pl.when(ci == 0)
    def _():
        cbuf[:, 0:DN_HALO, :] = jnp.zeros((nseq, DN_HALO, width), F32)
        if has_state:
            cbuf[:, lead:DN_HALO, :] = cst_ref[...]
            s_ref[...] = s0_ref[...]
        else:
            s_ref[...] = jnp.zeros(s_ref.shape, F32)

    @pl.when(ci > 0)
    def _():
        cbuf[:, 0:DN_HALO, :] = cbuf[:, seq_len:seq_len + DN_HALO, :]

    cbuf[:, DN_HALO:DN_HALO + seq_len, :] = x_ref[...].reshape(nseq, seq_len, width)
    conv = jnp.zeros((nseq, seq_len, width), F32)
    for j in range(DN_CONV):
        conv = conv + cw_ref[j] * cbuf[:, lead + j:lead + j + seq_len, :]
    qkv = _silu(conv).reshape(rows, width)

    @pl.when(ci == pl.num_programs(1) - 1)
    def _():
        nc_ref[...] = cbuf[:, seq_len + lead:seq_len + DN_HALO, :]

    ri = lax.broadcasted_iota(jnp.int32, (rows, rows), 0)
    cj = lax.broadcasted_iota(jnp.int32, (rows, rows), 1)
    shift = int(math.log2(seq_len))
    same = (ri >> shift) == (cj >> shift)
    lower = (ri >= cj) & same
    strict = (ri > cj) & same
    upper = (ri <= cj) & same
    eye = (ri == cj).astype(F32)
    row_seq = lax.broadcasted_iota(jnp.int32, (rows, 1), 0) >> shift
    ab = ab_ref[...]
    abt = abt_ref[...]
    hp = hp_ref[...]

    for h in range(DN_HEADS):
        q = qkv[:, h * dk:(h + 1) * dk]
        k = qkv[:, (DN_HEADS + h) * dk:(DN_HEADS + h + 1) * dk]
        v = qkv[:, (2 * DN_HEADS + h) * dk:(2 * DN_HEADS + h + 1) * dk]
        q = q * lax.rsqrt(jnp.sum(q * q, axis=-1, keepdims=True) + EPS) * (dk ** -0.5)
        k = k * lax.rsqrt(jnp.sum(k * k, axis=-1, keepdims=True) + EPS)
        neg_a = -jnp.exp(hp[0:1, h:h + 1])
        dt_bias = hp[1:2, h:h + 1]
        g_col = neg_a * _softplus(ab[:, h:h + 1] + dt_bias)
        g_row = neg_a * _softplus(abt[h:h + 1, :] + dt_bias)
        beta = _sigmoid(ab[:, DN_HEADS + h:DN_HEADS + h + 1])
        gc_col = jnp.sum(jnp.where(lower, g_row, 0.0), axis=1, keepdims=True)
        gc_row = jnp.sum(jnp.where(upper, g_col, 0.0), axis=0, keepdims=True)
        g_tot = jnp.sum(jnp.where(same, g_row, 0.0), axis=1, keepdims=True)
        decay = jnp.where(lower, jnp.exp(jnp.where(lower, gc_col - gc_row, 0.0)), 0.0)
        kb = k * beta
        vb = v * beta
        m = jnp.where(strict, _mm_nt(kb, k) * decay, 0.0)
        tinv = _unit_lower_inverse(m, eye, shift)
        u = _mm(tinv, vb)
        w = _mm(tinv, kb * jnp.exp(gc_col))
        qk = _mm_nt(q, k) * decay
        qg = q * jnp.exp(gc_col)
        kd = k * jnp.exp(g_tot - gc_col)
        e_tot = jnp.exp(g_tot)

        if nseq == 1:
            s_old = s_ref[0, h]
            v_new = u - _mm(w, s_old)
            o = _mm(qg, s_old) + _mm(qk, v_new)
            s_ref[0, h] = s_old * e_tot[0:1, :] + _mm_tn(kd, v_new)
        else:
            def read_state(s, carry):
                ws, qs = carry
                mine = row_seq == s
                s_old = s_ref[s, h]
                ws = ws + _mm(jnp.where(mine, w, 0.0), s_old)
                qs = qs + _mm(jnp.where(mine, qg, 0.0), s_old)
                return ws, qs

            zero = jnp.zeros((rows, dk), F32)
            ws, qs = lax.fori_loop(0, nseq, read_state, (zero, zero))
            v_new = u - ws
            o = qs + _mm(qk, v_new)

            def write_state(s, carry):
                mine = row_seq == s
                e_s = jnp.max(jnp.where(mine, e_tot, 0.0), axis=0, keepdims=True)
                s_ref[s, h] = s_ref[s, h] * e_s + _mm_tn(jnp.where(mine, kd, 0.0), v_new)
                return carry

            lax.fori_loop(0, nseq, write_state, 0)

        o = _rms_rows(o) * on_ref[...]
        g_ref[:, h * dk:(h + 1) * dk] = (o * _silu(z_ref[:, h * dk:(h + 1) * dk])).astype(g_ref.dtype)


def _gated_delta(proj, conv_state, s0, conv_w, a_log, dt_bias, o_norm, bsz, t, seq_len):
    n = bsz * t
    rows = DN_ROWS
    nseq = rows // seq_len
    n_chunk = max(t // rows, 1)
    n_grp = n // (rows * n_chunk)
    past = DN_CONV - 1
    has_state = conv_state is not None
    ab = proj[:, DN_QKV + D_MODEL:DN_QKV + D_MODEL + 2 * DN_HEADS]
    abt = ab.reshape(n // rows, rows, 2 * DN_HEADS).transpose(0, 2, 1)
    hp = jnp.zeros((2, LANES), F32).at[0, :DN_HEADS].set(a_log).at[1, :DN_HEADS].set(dt_bias)
    blk = lambda j: (lambda gi, ci: (gi * n_chunk + ci, j))
    per_grp3 = lambda gi, ci: (gi, 0, 0)
    per_grp4 = lambda gi, ci: (gi, 0, 0, 0)
    fixed2 = lambda gi, ci: (0, 0)
    in_specs = [pl.BlockSpec((rows, DN_QKV), blk(0)),
                pl.BlockSpec((rows, D_MODEL), blk(DN_QKV // D_MODEL)),
                pl.BlockSpec((rows, LANES), blk((DN_QKV + D_MODEL) // LANES)),
                pl.BlockSpec((None, 2 * DN_HEADS, rows), lambda gi, ci: (gi * n_chunk + ci, 0, 0))]
    args = [proj, proj, proj, abt]
    if has_state:
        in_specs += [pl.BlockSpec((nseq, past, DN_QKV), per_grp3),
                     pl.BlockSpec((nseq, DN_HEADS, DN_DK, DN_DK), per_grp4)]
        args += [conv_state, s0]
    in_specs += [pl.BlockSpec((DN_CONV, 1, DN_QKV), lambda gi, ci: (0, 0, 0)),
                 pl.BlockSpec((2, LANES), fixed2),
                 pl.BlockSpec((1, LANES), fixed2)]
    args += [conv_w.reshape(DN_CONV, 1, DN_QKV), hp, o_norm.reshape(1, LANES)]
    n_state = n_grp * nseq
    return pl.pallas_call(
        functools.partial(_delta_kernel, seq_len=seq_len, has_state=has_state),
        grid=(n_grp, n_chunk),
        in_specs=in_specs,
        out_specs=[pl.BlockSpec((rows, D_MODEL), blk(0)),
                   pl.BlockSpec((nseq, past, DN_QKV), per_grp3),
                   pl.BlockSpec((nseq, DN_HEADS, DN_DK, DN_DK), per_grp4)],
        out_shape=[jax.ShapeDtypeStruct((n, D_MODEL), BF16),
                   jax.ShapeDtypeStruct((n_state, past, DN_QKV), F32),
                   jax.ShapeDtypeStruct((n_state, DN_HEADS, DN_DK, DN_DK), F32)],
        scratch_shapes=[pltpu.VMEM((nseq, DN_HALO + seq_len, DN_QKV), F32)],
        compiler_params=_params(2),
        name="gated_delta",
    )(*args)


def kernel(x_prompt, x_sample, cache_k, cache_v, page_table, state_conv, state_delta_conv, state_delta_S, norm_gain, attn_w_in, attn_q_norm, attn_k_norm, attn_lambda_q, attn_lambda_k, attn_subln, attn_w_out, conv_w_in, conv_dw_w, conv_dw_b, conv_ln_g, conv_ln_b, conv_w_out, delta_w_in, delta_conv_w, delta_a_log, delta_dt_bias, delta_o_norm, delta_w_out):
    bp, tp, _ = x_prompt.shape
    bs, ts, _ = x_sample.shape
    past_len = page_table.shape[1] * PAGE_SIZE
    yp = x_prompt.reshape(bp * tp, D_MODEL)
    ys = x_sample.reshape(bs * ts, D_MODEL)

    cos_p, sin_p = _rope_tables(jnp.arange(tp))
    cos_s, sin_s = _rope_tables(past_len + jnp.arange(ts))
    cos_s = jnp.tile(cos_s, (ROW_TILE // ts, 1))
    sin_s = jnp.tile(sin_s, (ROW_TILE // ts, 1))

    kp_l, vp_l, ks_l, vs_l = [], [], [], []
    cp_l, cs_l = [], []
    dcp_l, dcs_l, dsp_l, dss_l = [], [], [], []
    for i in range(DEPTH):
        kind = i % N_MIXERS
        j = i // N_MIXERS
        gain = norm_gain[i]
        if kind == 0:
            lam_init = 0.8 - 0.6 * math.exp(-0.3 * i)
            w_in = attn_w_in[j].astype(BF16)
            w_out = attn_w_out[j].astype(BF16)
            lq, lk = attn_lambda_q[j], attn_lambda_k[j]
            qb, k, v, z, kb, vb = _attn_proj(yp, gain, w_in, attn_q_norm[j], attn_k_norm[j], cos_p, sin_p, BF16)
            gp = _flash_attention(qb, kb, vb, z, lq, lk, attn_subln[j], lam_init, bp, tp)
            kp_l.append(k.reshape(bp, tp, ATT_HEADS, 2, ATT_DH))
            vp_l.append(v.reshape(bp, tp, ATT_HEADS, ATT_VD))
            yp = _out_proj(gp, w_out, yp)
            q, k, v, z, _, _ = _attn_proj(ys, gain, w_in, attn_q_norm[j], attn_k_norm[j], cos_s, sin_s, F32)
            gs = _sample_attention(q, k, v, z, cache_k, cache_v, j, page_table, lq, lk, attn_subln[j], lam_init)
            ks_l.append(k.reshape(bs, ts, ATT_HEADS, 2, ATT_DH))
            vs_l.append(v.reshape(bs, ts, ATT_HEADS, ATT_VD))
            ys = _out_proj(gs, w_out, ys)
        elif kind == 1:
            w_in = conv_w_in[j].astype(BF16)
            w_out = conv_w_out[j].astype(BF16)
            tail = (conv_dw_w[j], conv_dw_b[j], conv_ln_g[j], conv_ln_b[j], w_out)
            proj = _norm_proj(yp, gain, w_in).reshape(bp, tp, 3 * D_MODEL)
            y3, buf = _conformer(proj, yp.reshape(bp, tp, D_MODEL), None, *tail, nseq=1, tt=CONV_TILE)
            yp = y3.reshape(bp * tp, D_MODEL)
            cp_l.append(buf)
            proj = _norm_proj(ys, gain, w_in).reshape(bs, ts, 3 * D_MODEL)
            y3, buf = _conformer(proj, ys.reshape(bs, ts, D_MODEL), state_conv[j], *tail,
                                 nseq=DN_ROWS // ts, tt=ts)
            ys = y3.reshape(bs * ts, D_MODEL)
            cs_l.append(buf)
        else:
            n_in = delta_w_in.shape[2]
            n_pad = -n_in % LANES
            w_in = jnp.pad(delta_w_in[j], ((0, 0), (0, n_pad))).astype(BF16)
            w_out = delta_w_out[j].astype(BF16)
            tail = (delta_conv_w[j], delta_a_log[j], delta_dt_bias[j], delta_o_norm[j])
            proj = _norm_proj(yp, gain, w_in)
            g, buf, s_new = _gated_delta(proj, None, None, *tail, bsz=bp, t=tp, seq_len=DN_ROWS)
            yp = _out_proj(g, w_out, yp)
            dcp_l.append(buf)
            dsp_l.append(s_new)
            proj = _norm_proj(ys, gain, w_in)
            g, buf, s_new = _gated_delta(proj, state_delta_conv[j], state_delta_S[j], *tail,
                                         bsz=bs, t=ts, seq_len=ts)
            ys = _out_proj(g, w_out, ys)
            dcs_l.append(buf)
            dss_l.append(s_new)
    return (yp.reshape(bp, tp, D_MODEL), ys.reshape(bs, ts, D_MODEL),
            jnp.stack(kp_l), jnp.stack(vp_l), jnp.stack(ks_l), jnp.stack(vs_l),
            jnp.stack(cp_l), jnp.stack(cs_l), jnp.stack(dcp_l), jnp.stack(dcs_l),
            jnp.stack(dsp_l), jnp.stack(dss_l))
```

```python
import functools
import math

import jax
import jax.numpy as jnp
from jax import lax
from jax.experimental import pallas as pl
from jax.experimental.pallas import tpu as pltpu

F32 = jnp.float32
BF16 = jnp.bfloat16

D_MODEL = 1024
DEPTH = 4
N_MIXERS = 3
PAGE_SIZE = 128
ATT_HEADS = 8
ATT_DH = 64
ATT_VD = 128
ROPE_THETA = 10000.0
NEG_INF = -1e30
CONV_WIDTH = 31
DN_HEADS = 8
DN_DK = 128
DN_QKV = 3072
DN_CONV = 4
EPS = 1e-6

LANES = 128
ROW_TILE = 256
ATT_BLOCK = 512
PAGES_PER_STEP = 8
CONV_TILE = 256
CONV_HALO = 32
DN_ROWS = 128
DN_HALO = 8
VMEM_LIMIT = 56 * 1024 * 1024


def _params(n_axes):
    return pltpu.CompilerParams(dimension_semantics=("arbitrary",) * n_axes,
                                vmem_limit_bytes=VMEM_LIMIT)


def _mm(a, b):
    return jnp.dot(a.astype(BF16), b.astype(BF16), preferred_element_type=F32)


def _mm_nt(a, b):
    return lax.dot_general(a.astype(BF16), b.astype(BF16), (((1,), (1,)), ((), ())),
                           preferred_element_type=F32)


def _mm_tn(a, b):
    return lax.dot_general(a.astype(BF16), b.astype(BF16), (((0,), (0,)), ((), ())),
                           preferred_element_type=F32)


def _sigmoid(x):
    return 1.0 / (1.0 + jnp.exp(-x))


def _silu(x):
    return x * _sigmoid(x)


def _softplus(x):
    return jnp.maximum(x, 0.0) + jnp.log(1.0 + jnp.exp(-jnp.abs(x)))


def _rms_rows(x):
    return x * lax.rsqrt(jnp.mean(x * x, axis=-1, keepdims=True) + EPS)


def _norm_proj_kernel(x_ref, g_ref, w_ref, o_ref, *, col_chunk):
    h = (_rms_rows(x_ref[...]) * g_ref[...]).astype(BF16)
    n_out = o_ref.shape[1]
    for c0 in range(0, n_out, col_chunk):
        c1 = min(c0 + col_chunk, n_out)
        o_ref[:, c0:c1] = jnp.dot(h, w_ref[:, c0:c1], preferred_element_type=F32)


def _norm_proj(x, gain, w_bf16):
    n, n_out = x.shape[0], w_bf16.shape[1]
    return pl.pallas_call(
        functools.partial(_norm_proj_kernel, col_chunk=1024),
        grid=(n // ROW_TILE,),
        in_specs=[pl.BlockSpec((ROW_TILE, D_MODEL), lambda i: (i, 0)),
                  pl.BlockSpec((1, D_MODEL), lambda i: (0, 0)),
                  pl.BlockSpec((D_MODEL, n_out), lambda i: (0, 0))],
        out_specs=pl.BlockSpec((ROW_TILE, n_out), lambda i: (i, 0)),
        out_shape=jax.ShapeDtypeStruct((n, n_out), F32),
        compiler_params=_params(1),
        name="norm_proj",
    )(x, gain.reshape(1, D_MODEL), w_bf16)


def _attn_proj_kernel(x_ref, g_ref, w_ref, qn_ref, kn_ref, cos_ref, sin_ref,
                      q_ref, k_ref, v_ref, z_ref, kb_ref, vb_ref):
    h = (_rms_rows(x_ref[...]) * g_ref[...]).astype(BF16)
    width = ATT_HEADS * ATT_VD
    r = lax.broadcasted_iota(jnp.int32, (LANES, LANES), 0)
    c = lax.broadcasted_iota(jnp.int32, (LANES, LANES), 1)
    group_ones = ((r >> 6) == (c >> 6)).astype(BF16)
    lane = lax.broadcasted_iota(jnp.int32, (1, LANES), 1)
    first_half = (lane & (ATT_DH - 1)) < (ATT_DH // 2)
    cos = cos_ref[...]
    sin = sin_ref[...]

    def norm_rope(raw, gain):
        ms = jnp.dot((raw * raw).astype(BF16), group_ones, preferred_element_type=F32) * (1.0 / ATT_DH)
        xn = raw * lax.rsqrt(ms + EPS) * gain
        rot = jnp.where(first_half, pltpu.roll(xn, LANES - ATT_DH // 2, 1), pltpu.roll(xn, ATT_DH // 2, 1))
        return xn * cos + rot * sin

    pair = 2 * LANES
    for c0 in range(0, width, pair):
        q_raw = jnp.dot(h, w_ref[:, c0:c0 + pair], preferred_element_type=F32)
        k_raw = jnp.dot(h, w_ref[:, width + c0:width + c0 + pair], preferred_element_type=F32)
        for s in range(0, pair, LANES):
            sl = slice(c0 + s, c0 + s + LANES)
            q = norm_rope(q_raw[:, s:s + LANES], qn_ref[...]) * (ATT_DH ** -0.5)
            q_ref[:, sl] = q.astype(q_ref.dtype)
            k = norm_rope(k_raw[:, s:s + LANES], kn_ref[...])
            k_ref[:, sl] = k
            kb_ref[:, sl] = k.astype(BF16)
    v = jnp.dot(h, w_ref[:, 2 * width:3 * width], preferred_element_type=F32)
    v_ref[...] = v
    vb_ref[...] = v.astype(BF16)
    z_ref[...] = jnp.dot(h, w_ref[:, 3 * width:4 * width], preferred_element_type=F32)


def _attn_proj(x, gain, w_bf16, qn, kn, cos, sin, q_dtype):
    n = x.shape[0]
    n_tab = cos.shape[0] // ROW_TILE
    width = ATT_HEADS * ATT_VD
    row = lambda i: (i, 0)
    fixed = lambda i: (0, 0)
    out = lambda dt: jax.ShapeDtypeStruct((n, width), dt)
    return pl.pallas_call(
        _attn_proj_kernel,
        grid=(n // ROW_TILE,),
        in_specs=[pl.BlockSpec((ROW_TILE, D_MODEL), row),
                  pl.BlockSpec((1, D_MODEL), fixed),
                  pl.BlockSpec((D_MODEL, 4 * width), fixed),
                  pl.BlockSpec((1, LANES), fixed),
                  pl.BlockSpec((1, LANES), fixed),
                  pl.BlockSpec((ROW_TILE, LANES), lambda i: (i % n_tab, 0)),
                  pl.BlockSpec((ROW_TILE, LANES), lambda i: (i % n_tab, 0))],
        out_specs=[pl.BlockSpec((ROW_TILE, width), row)] * 6,
        out_shape=[out(q_dtype), out(F32), out(F32), out(F32), out(BF16), out(BF16)],
        compiler_params=_params(1),
        name="attn_proj",
    )(x, gain.reshape(1, D_MODEL), w_bf16, jnp.tile(qn, 2).reshape(1, LANES), jnp.tile(kn, 2).reshape(1, LANES),
      cos, sin)


def _rope_tables(pos):
    inv = ROPE_THETA ** (-jnp.arange(0, ATT_DH, 2, dtype=F32) / ATT_DH)
    ang = pos.astype(F32)[:, None] * inv[None, :]
    cos = jnp.cos(ang)
    sin = jnp.sin(ang)
    cos = jnp.concatenate([cos, cos, cos, cos], axis=-1)
    sin = jnp.concatenate([-sin, sin, -sin, sin], axis=-1)
    return cos, sin


def _diff_lambda_in_kernel(lq_ref, lk_ref, lam_init):
    e = jnp.exp(jnp.sum(lq_ref[...] * lk_ref[...], axis=1, keepdims=True))
    return e[0:1] - e[1:2] + lam_init


def _diff_finish(acc1, l1, acc2, l2, lam, lam_init, subln, z):
    o = acc1 / l1 - lam * (acc2 / l2)
    o = _rms_rows(o) * subln * (1.0 - lam_init)
    return o * _silu(z)


def _flash_kernel(q_ref, k_ref, v_ref, z_ref, lq_ref, lk_ref, sub_ref, o_ref, *, lam_init):
    blk = ATT_BLOCK
    nblk = q_ref.shape[0] // blk
    lam = _diff_lambda_in_kernel(lq_ref, lk_ref, lam_init)
    lane = lax.broadcasted_iota(jnp.int32, (1, LANES), 1)
    kv_i = lax.broadcasted_iota(jnp.int32, (blk, 2 * blk), 0)
    q_i = lax.broadcasted_iota(jnp.int32, (blk, 2 * blk), 1) & (blk - 1)
    diag_visible = q_i >= kv_i
    for qi in range(nblk):
        rows = slice(qi * blk, (qi + 1) * blk)
        q = q_ref[rows, :]
        zero = jnp.zeros_like(q)
        q_maps = jnp.concatenate([jnp.where(lane < ATT_DH, q, zero), jnp.where(lane >= ATT_DH, q, zero)], axis=0)
        m = jnp.full((1, 2 * blk), NEG_INF, F32)
        l = jnp.zeros((1, 2 * blk), F32)
        acc = jnp.zeros((ATT_VD, 2 * blk), F32)
        for ki in range(qi + 1):
            kv_rows = slice(ki * blk, (ki + 1) * blk)
            s = lax.dot_general(k_ref[kv_rows, :], q_maps, (((1,), (1,)), ((), ())), preferred_element_type=F32)
            if ki == qi:
                s = jnp.where(diag_visible, s, NEG_INF)
            m_new = jnp.maximum(m, jnp.max(s, axis=0, keepdims=True))
            alpha = jnp.exp(m - m_new)
            p = jnp.exp(s - m_new)
            l = alpha * l + jnp.sum(p, axis=0, keepdims=True)
            pv = lax.dot_general(v_ref[kv_rows, :], p.astype(BF16), (((0,), (0,)), ((), ())),
                                 preferred_element_type=F32)
            acc = alpha * acc + pv
            m = m_new
        o_t = acc[:, :blk] / l[:, :blk] - lam * (acc[:, blk:] / l[:, blk:])
        o_t = o_t * lax.rsqrt(jnp.mean(o_t * o_t, axis=0, keepdims=True) + EPS) * sub_ref[...] * (1.0 - lam_init)
        o_ref[rows, :] = (o_t.T * _silu(z_ref[rows, :])).astype(o_ref.dtype)


def _flash_attention(qb, kb, vb, z, lq, lk, subln, lam_init, batch, seq):
    n = batch * seq
    per_head = lambda b, h: (b, h)
    fixed = lambda b, h: (0, 0)
    return pl.pallas_call(
        functools.partial(_flash_kernel, lam_init=lam_init),
        grid=(batch, ATT_HEADS),
        in_specs=[pl.BlockSpec((seq, LANES), per_head),
                  pl.BlockSpec((seq, LANES), per_head),
                  pl.BlockSpec((seq, LANES), per_head),
                  pl.BlockSpec((seq, LANES), per_head),
                  pl.BlockSpec((2, ATT_DH), fixed),
                  pl.BlockSpec((2, ATT_DH), fixed),
                  pl.BlockSpec((ATT_VD, 1), fixed)],
        out_specs=pl.BlockSpec((seq, LANES), per_head),
        out_shape=jax.ShapeDtypeStruct((n, ATT_HEADS * ATT_VD), BF16),
        compiler_params=_params(2),
        name="flash_diff_attention",
    )(qb, kb, vb, z, lq, lk, subln.reshape(ATT_VD, 1))


def _sample_attn_kernel(*refs, lam_init, n_steps):
    g = PAGES_PER_STEP
    q_ref = refs[1]
    kt_refs = refs[2:2 + g]
    v_refs = refs[2 + g:2 + 2 * g]
    kn_ref, vn_ref, z_ref, lq_ref, lk_ref, sub_ref, o_ref, qbd_sc, m_sc, l_sc, acc_sc = refs[2 + 2 * g:]
    p = pl.program_id(1)
    t = q_ref.shape[0]
    width = q_ref.shape[1]
    n_rows = 2 * ATT_HEADS * t
    per_head = 2 * t

    @pl.when(p == 0)
    def _():
        q = jnp.concatenate([q_ref[...]] * (2 * ATT_HEADS), axis=0)
        r = lax.broadcasted_iota(jnp.int32, (n_rows, width), 0)
        c = lax.broadcasted_iota(jnp.int32, (n_rows, width), 1)
        qbd_sc[...] = jnp.where((r >> 3) == (c >> 6), q, 0.0).astype(BF16)
        m_sc[...] = jnp.full(m_sc.shape, NEG_INF, F32)
        l_sc[...] = jnp.zeros(l_sc.shape, F32)
        acc_sc[...] = jnp.zeros(acc_sc.shape, F32)

    def update(s, v_of_head):
        m_old = m_sc[...]
        m_new = jnp.maximum(m_old, jnp.max(s, axis=1, keepdims=True))
        alpha = jnp.exp(m_old - m_new)
        e = jnp.exp(s - m_new)
        l_sc[...] = alpha * l_sc[...] + jnp.sum(e, axis=1, keepdims=True)
        m_sc[...] = m_new
        e = e.astype(BF16)
        for h in range(ATT_HEADS):
            rows = slice(h * per_head, (h + 1) * per_head)
            acc_sc[rows, :] = alpha[rows] * acc_sc[rows, :] + jnp.dot(e[rows], v_of_head(h),
                                                                      preferred_element_type=F32)

    kt = jnp.concatenate([r[...].astype(BF16) for r in kt_refs], axis=1)

    def cached_v(h):
        return jnp.concatenate([r[pl.ds(h, PAGE_SIZE, stride=ATT_HEADS), :].astype(BF16) for r in v_refs], axis=0)

    update(jnp.dot(qbd_sc[...], kt, preferred_element_type=F32), cached_v)

    @pl.when(p == n_steps - 1)
    def _():
        pad = n_rows - t
        kn = jnp.concatenate([kn_ref[...], jnp.zeros((pad, width), F32)], axis=0).astype(BF16)
        s = lax.dot_general(qbd_sc[...], kn, (((1,), (1,)), ((), ())), preferred_element_type=F32)
        row = lax.broadcasted_iota(jnp.int32, (n_rows, n_rows), 0)
        col = lax.broadcasted_iota(jnp.int32, (n_rows, n_rows), 1)
        s = jnp.where(col <= (row & (t - 1)), s, NEG_INF)

        def new_v(h):
            vh = vn_ref[:, h * ATT_VD:(h + 1) * ATT_VD]
            return jnp.concatenate([vh, jnp.zeros((pad, ATT_VD), F32)], axis=0).astype(BF16)

        update(s, new_v)
        lam = _diff_lambda_in_kernel(lq_ref, lk_ref, lam_init)
        acc = acc_sc[...]
        l = l_sc[...]
        for h in range(ATT_HEADS):
            r0 = h * per_head
            cols = slice(h * ATT_VD, (h + 1) * ATT_VD)
            o_ref[:, cols] = _diff_finish(acc[r0:r0 + t], l[r0:r0 + t], acc[r0 + t:r0 + 2 * t],
                                          l[r0 + t:r0 + 2 * t], lam, lam_init, sub_ref[...], z_ref[:, cols])


def _sample_attention(q, k_new, v_new, z, cache_k, cache_v, layer, page_table, lq, lk, subln, lam_init):
    batch, n_pages = page_table.shape
    t = q.shape[0] // batch
    h, dh, vd = ATT_HEADS, ATT_DH, ATT_VD
    width = h * vd
    g = PAGES_PER_STEP
    n_steps = n_pages // g
    assert t == 8 and n_pages % g == 0 and cache_k.shape[2] == PAGE_SIZE
    n_layer, n_pool = cache_k.shape[:2]
    ckt = cache_k.transpose(0, 1, 3, 4, 5, 2).reshape(n_layer, n_pool, width, PAGE_SIZE)
    cv = cache_v.reshape(n_layer, n_pool, PAGE_SIZE * h, vd)
    pt = page_table.reshape(-1)

    def page(i):
        return lambda b, p, pt_ref: (layer, pt_ref[b * n_pages + p * g + i], 0, 0)

    per_b = lambda b, p, pt_ref: (b, 0)
    fixed = lambda b, p, pt_ref: (0, 0)
    grid_spec = pltpu.PrefetchScalarGridSpec(
        num_scalar_prefetch=1,
        grid=(batch, n_steps),
        in_specs=([pl.BlockSpec((t, width), per_b)]
                  + [pl.BlockSpec((None, None, width, PAGE_SIZE), page(i)) for i in range(g)]
                  + [pl.BlockSpec((None, None, PAGE_SIZE * h, vd), page(i)) for i in range(g)]
                  + [pl.BlockSpec((t, width), per_b),
                     pl.BlockSpec((t, width), per_b),
                     pl.BlockSpec((t, width), per_b),
                     pl.BlockSpec((2, dh), fixed),
                     pl.BlockSpec((2, dh), fixed),
                     pl.BlockSpec((1, vd), fixed)]),
        out_specs=pl.BlockSpec((t, width), per_b),
        scratch_shapes=[pltpu.VMEM((2 * h * t, width), BF16),
                        pltpu.VMEM((2 * h * t, 1), F32),
                        pltpu.VMEM((2 * h * t, 1), F32),
                        pltpu.VMEM((2 * h * t, vd), F32)],
    )
    return pl.pallas_call(
        functools.partial(_sample_attn_kernel, lam_init=lam_init, n_steps=n_steps),
        grid_spec=grid_spec,
        out_shape=jax.ShapeDtypeStruct((batch * t, width), F32),
        compiler_params=_params(2),
        name="paged_diff_attention",
    )(pt, q, *([ckt] * g), *([cv] * g), k_new, v_new, z, lq, lk, subln.reshape(1, vd))


def _out_proj_kernel(g_ref, w_ref, x_ref, y_ref):
    y_ref[...] = x_ref[...] + jnp.dot(g_ref[...].astype(BF16), w_ref[...], preferred_element_type=F32)


def _out_proj(g, w_bf16, x):
    n = x.shape[0]
    row = lambda i: (i, 0)
    return pl.pallas_call(
        _out_proj_kernel,
        grid=(n // ROW_TILE,),
        in_specs=[pl.BlockSpec((ROW_TILE, g.shape[1]), row),
                  pl.BlockSpec(w_bf16.shape, lambda i: (0, 0)),
                  pl.BlockSpec((ROW_TILE, D_MODEL), row)],
        out_specs=pl.BlockSpec((ROW_TILE, D_MODEL), row),
        out_shape=jax.ShapeDtypeStruct((n, D_MODEL), F32),
        compiler_params=_params(1),
        name="out_proj",
    )(g, w_bf16, x)


def _conformer_kernel(*refs, has_state):
    if has_state:
        (a_ref, b_ref, z_ref, x_ref, st_ref, dw_ref, db_ref, lg_ref, lb_ref, w_ref,
         y_ref, nb_ref, ubuf) = refs
    else:
        (a_ref, b_ref, z_ref, x_ref, dw_ref, db_ref, lg_ref, lb_ref, w_ref,
         y_ref, nb_ref, ubuf) = refs
    ti = pl.program_id(1)
    nseq, tt, ch = a_ref.shape
    past = CONV_WIDTH - 1
    lead = CONV_HALO - past

    @pl.when(ti == 0)
    def _():
        ubuf[:, 0:CONV_HALO, :] = jnp.zeros((nseq, CONV_HALO, ch), F32)
        if has_state:
            ubuf[:, lead:CONV_HALO, :] = st_ref[...]

    @pl.when(ti > 0)
    def _():
        ubuf[:, 0:CONV_HALO, :] = ubuf[:, tt:tt + CONV_HALO, :]

    ubuf[:, CONV_HALO:CONV_HALO + tt, :] = a_ref[...] * _sigmoid(b_ref[...])
    sub = 8
    acc = jnp.zeros((nseq, tt, ch), F32) + db_ref[...]
    for s in range(sub):
        taps = [j for j in range(CONV_WIDTH) if (lead + j) % sub == s]
        part = None
        n_win = tt + (sub if s else 0)
        for j in taps:
            base = lead + j - s
            term = dw_ref[j] * ubuf[:, base:base + n_win, :]
            part = term if part is None else part + term
        acc = acc + part[:, s:s + tt, :]
    mean = jnp.mean(acc, axis=-1, keepdims=True)
    cen = acc - mean
    var = jnp.mean(cen * cen, axis=-1, keepdims=True)
    c = _silu(cen * lax.rsqrt(var + EPS) * lg_ref[...] + lb_ref[...])
    gated = (c * _silu(z_ref[...])).reshape(nseq * tt, ch).astype(BF16)
    y = jnp.dot(gated, w_ref[...], preferred_element_type=F32)
    y_ref[...] = x_ref[...] + y.reshape(nseq, tt, ch)

    @pl.when(ti == pl.num_programs(1) - 1)
    def _():
        nb_ref[...] = ubuf[:, tt + lead:tt + CONV_HALO, :]


def _conformer(proj, x, state, dw_w, dw_b, ln_g, ln_b, w_out_bf16, nseq, tt):
    bsz, t, _ = x.shape
    ch = D_MODEL
    past = CONV_WIDTH - 1
    col = lambda j: (lambda bi, ti: (bi, ti, j))
    fixed2 = lambda bi, ti: (0, 0)
    in_specs = [pl.BlockSpec((nseq, tt, ch), col(0)),
                pl.BlockSpec((nseq, tt, ch), col(1)),
                pl.BlockSpec((nseq, tt, ch), col(2)),
                pl.BlockSpec((nseq, tt, ch), col(0))]
    args = [proj, proj, proj, x]
    if state is not None:
        in_specs.append(pl.BlockSpec((nseq, past, ch), lambda bi, ti: (bi, 0, 0)))
        args.append(state)
    in_specs += [pl.BlockSpec((CONV_WIDTH, 1, ch), lambda bi, ti: (0, 0, 0)),
                 pl.BlockSpec((1, ch), fixed2),
                 pl.BlockSpec((1, ch), fixed2),
                 pl.BlockSpec((1, ch), fixed2),
                 pl.BlockSpec((ch, D_MODEL), fixed2)]
    args += [dw_w.reshape(CONV_WIDTH, 1, ch), dw_b.reshape(1, ch), ln_g.reshape(1, ch), ln_b.reshape(1, ch),
             w_out_bf16]
    return pl.pallas_call(
        functools.partial(_conformer_kernel, has_state=state is not None),
        grid=(bsz // nseq, t // tt),
        in_specs=in_specs,
        out_specs=[pl.BlockSpec((nseq, tt, D_MODEL), col(0)),
                   pl.BlockSpec((nseq, past, ch), lambda bi, ti: (bi, 0, 0))],
        out_shape=[jax.ShapeDtypeStruct((bsz, t, D_MODEL), F32),
                   jax.ShapeDtypeStruct((bsz, past, ch), F32)],
        scratch_shapes=[pltpu.VMEM((nseq, CONV_HALO + tt, ch), F32)],
        compiler_params=_params(2),
        name="conformer_conv",
    )(*args)


def _unit_lower_inverses(ms, eye, n_stage):
    n = eye.shape[0]
    dot = functools.partial(jnp.dot, preferred_element_type=F32)
    powers = [-m for m in ms]
    invs = [eye + p for p in powers]
    powers = [_mm(p, p) for p in powers]
    for _ in range(n_stage - 2):
        both = [_mm(jnp.concatenate([p, inv], axis=0), p) for p, inv in zip(powers, invs)]
        invs = [inv + b[n:] for inv, b in zip(invs, both)]
        powers = [b[:n] for b in both]
    invs = [inv + _mm(inv, p) for inv, p in zip(invs, powers)]
    mh = [m.astype(BF16) for m in ms]
    ml = [(m - h.astype(F32)).astype(BF16) for m, h in zip(ms, mh)]
    ih = [inv.astype(BF16) for inv in invs]
    il = [(inv - h.astype(F32)).astype(BF16) for inv, h in zip(invs, ih)]
    hi = [dot(jnp.concatenate([a, b], axis=0), c) for a, b, c in zip(mh, ml, ih)]
    lo = [dot(a, c) for a, c in zip(mh, il)]
    resid = [eye - inv - (h[:n] + h[n:] + l) for inv, h, l in zip(invs, hi, lo)]
    return [inv + _mm(inv, r) for inv, r in zip(invs, resid)]


def _delta_kernel(*refs, seq_len, has_state):
    if has_state:
        (x_ref, z_ref, ab_ref, abt_ref, cst_ref, s0_ref, cw_ref, hp_ref, on_ref,
         g_ref, nc_ref, s_ref, cbuf) = refs
    else:
        (x_ref, z_ref, ab_ref, abt_ref, cw_ref, hp_ref, on_ref,
         g_ref, nc_ref, s_ref, cbuf) = refs
    ci = pl.program_id(1)
    rows, width = x_ref.shape
    nseq = rows // seq_len
    past = DN_CONV - 1
    lead = DN_HALO - past
    dk = DN_DK

    @pl.when(ci == 0)
    def _():
        cbuf[:, 0:DN_HALO, :] = jnp.zeros((nseq, DN_HALO, width), F32)
        if has_state:
            cbuf[:, lead:DN_HALO, :] = cst_ref[...]
            s_ref[...] = s0_ref[...]
        else:
            s_ref[...] = jnp.zeros(s_ref.shape, F32)

    @pl.when(ci > 0)
    def _():
        cbuf[:, 0:DN_HALO, :] = cbuf[:, seq_len:seq_len + DN_HALO, :]

    cbuf[:, DN_HALO:DN_HALO + seq_len, :] = x_ref[...].reshape(nseq, seq_len, width)
    conv = jnp.zeros((nseq, seq_len, width), F32)
    for j in range(DN_CONV):
        conv = conv + cw_ref[j] * cbuf[:, lead + j:lead + j + seq_len, :]
    qkv = _silu(conv).reshape(rows, width)

    @pl.when(ci == pl.num_programs(1) - 1)
    def _():
        nc_ref[...] = cbuf[:, seq_len + lead:seq_len + DN_HALO, :]

    pair = 2 * rows
    ri = lax.broadcasted_iota(jnp.int32, (pair, pair), 0)
    cj = lax.broadcasted_iota(jnp.int32, (pair, pair), 1)
    shift = int(math.log2(seq_len))
    head_shift = int(math.log2(rows))
    same = (ri >> shift) == (cj >> shift)
    lower = (ri >= cj) & same
    strict = (ri > cj) & same
    upper = (ri <= cj) & same
    eye = (ri == cj).astype(F32)
    ab = ab_ref[...]
    abt = abt_ref[...]
    hp = hp_ref[...]

    def stack(fn, heads):
        return jnp.concatenate([fn(h) for h in heads], axis=0)

    def block_diag(x):
        r = lax.broadcasted_iota(jnp.int32, (x.shape[0], 2 * dk), 0)
        c = lax.broadcasted_iota(jnp.int32, (x.shape[0], 2 * dk), 1)
        own = ((r >> head_shift) & 1) == (c >> int(math.log2(dk)))
        return jnp.where(own, jnp.concatenate([x, x], axis=1), 0.0)

    def seq_of(n):
        r = lax.broadcasted_iota(jnp.int32, (n, 1), 0)
        return (r & (rows - 1)) >> shift

    first_heads = list(range(0, DN_HEADS, 2))

    def prepare(h0):
        heads = (h0, h0 + 1)
        q = stack(lambda h: qkv[:, h * dk:(h + 1) * dk], heads)
        k = stack(lambda h: qkv[:, (DN_HEADS + h) * dk:(DN_HEADS + h + 1) * dk], heads)
        v = stack(lambda h: qkv[:, (2 * DN_HEADS + h) * dk:(2 * DN_HEADS + h + 1) * dk], heads)
        q = q * lax.rsqrt(jnp.sum(q * q, axis=-1, keepdims=True) + EPS) * (dk ** -0.5)
        k = k * lax.rsqrt(jnp.sum(k * k, axis=-1, keepdims=True) + EPS)
        neg_a = lambda h: -jnp.exp(hp[0:1, h:h + 1])
        dt_bias = lambda h: hp[1:2, h:h + 1]
        g_col = stack(lambda h: neg_a(h) * _softplus(ab[:, h:h + 1] + dt_bias(h)), heads)
        g_row = jnp.concatenate([neg_a(h) * _softplus(abt[h:h + 1, :] + dt_bias(h)) for h in heads], axis=1)
        beta = stack(lambda h: _sigmoid(ab[:, DN_HEADS + h:DN_HEADS + h + 1]), heads)
        gc_col = jnp.sum(jnp.where(lower, g_row, 0.0), axis=1, keepdims=True)
        gc_row = jnp.sum(jnp.where(upper, g_col, 0.0), axis=0, keepdims=True)
        g_tot = jnp.sum(jnp.where(same, g_row, 0.0), axis=1, keepdims=True)
        decay = jnp.where(lower, jnp.exp(jnp.where(lower, gc_col - gc_row, 0.0)), 0.0)
        kb = k * beta
        kbq = jnp.concatenate([kb, q], axis=0)
        rhs = jnp.concatenate([v * beta, kb * jnp.exp(gc_col)], axis=1)
        qg = q * jnp.exp(gc_col)
        kd = block_diag(k * jnp.exp(g_tot - gc_col))
        return dict(k=k, kbq=kbq, rhs=rhs, qg=qg, kd=kd, decay=decay, e_tot=jnp.exp(g_tot))

    pre = [prepare(h0) for h0 in first_heads]
    kk_qk = [_mm_nt(p["kbq"], p["k"]) for p in pre]
    ms = [jnp.where(strict, x[:pair] * p["decay"], 0.0) for x, p in zip(kk_qk, pre)]
    qks = [x[pair:] * p["decay"] for x, p in zip(kk_qk, pre)]
    tinvs = _unit_lower_inverses(ms, eye, shift)
    u_ws = [_mm(t, p["rhs"]) for t, p in zip(tinvs, pre)]
    us = [x[:, :dk] for x in u_ws]
    w_qgs = [block_diag(jnp.concatenate([x[:, dk:], p["qg"]], axis=0)) for x, p in zip(u_ws, pre)]
    kds = [p["kd"] for p in pre]
    e_tots = [p["e_tot"] for p in pre]

    def state_rows(e_of_head):
        return jnp.concatenate([jnp.broadcast_to(e_of_head(i), (dk, 1)) for i in range(2)], axis=0)

    def load_state(s, h0):
        return s_ref[s, h0:h0 + 2].reshape(2 * dk, dk)

    if nseq == 1:
        s_olds = [load_state(0, h0) for h0 in first_heads]
        ws_qs = [_mm(a, s_old) for a, s_old in zip(w_qgs, s_olds)]
        v_news = [u - x[:pair] for u, x in zip(us, ws_qs)]
        os_ = [x[pair:] + _mm(qk, v_new) for x, qk, v_new in zip(ws_qs, qks, v_news)]
        for h0, s_old, e_tot, kd, v_new in zip(first_heads, s_olds, e_tots, kds, v_news):
            scale = state_rows(lambda i: e_tot[i * rows:i * rows + 1])
            s_ref[0, h0:h0 + 2] = (s_old * scale + _mm_tn(kd, v_new)).reshape(2, dk, dk)
    else:
        seq2 = seq_of(2 * pair)
        seq1 = seq_of(pair)
        head1 = lax.broadcasted_iota(jnp.int32, (pair, 1), 0) >> head_shift

        def read_state(s, accs):
            return tuple(acc + _mm(jnp.where(seq2 == s, a, 0.0), load_state(s, h0))
                         for acc, a, h0 in zip(accs, w_qgs, first_heads))

        ws_qs = lax.fori_loop(0, nseq, read_state,
                              tuple(jnp.zeros((2 * pair, dk), F32) for _ in first_heads))
        v_news = [u - x[:pair] for u, x in zip(us, ws_qs)]
        os_ = [x[pair:] + _mm(qk, v_new) for x, qk, v_new in zip(ws_qs, qks, v_news)]

        def write_state(s, carry):
            mine = seq1 == s
            for h0, e_tot, kd, v_new in zip(first_heads, e_tots, kds, v_news):
                scale = state_rows(
                    lambda i: jnp.max(jnp.where(mine & (head1 == i), e_tot, 0.0), axis=0, keepdims=True))
                s_new = load_state(s, h0) * scale + _mm_tn(jnp.where(mine, kd, 0.0), v_new)
                s_ref[s, h0:h0 + 2] = s_new.reshape(2, dk, dk)
            return carry

        lax.fori_loop(0, nseq, write_state, 0)

    for h0, o in zip(first_heads, os_):
        o = _rms_rows(o) * on_ref[...]
        for i in range(2):
            cols = slice((h0 + i) * dk, (h0 + i + 1) * dk)
            g_ref[:, cols] = (o[i * rows:(i + 1) * rows] * _silu(z_ref[:, cols])).astype(g_ref.dtype)


def _gated_delta(proj, conv_state, s0, conv_w, a_log, dt_bias, o_norm, bsz, t, seq_len):
    n = bsz * t
    rows = DN_ROWS
    nseq = rows // seq_len
    n_chunk = max(t // rows, 1)
    n_grp = n // (rows * n_chunk)
    past = DN_CONV - 1
    has_state = conv_state is not None
    ab = proj[:, DN_QKV + D_MODEL:DN_QKV + D_MODEL + 2 * DN_HEADS]
    abt = ab.reshape(n // rows, rows, 2 * DN_HEADS).transpose(0, 2, 1)
    hp = jnp.zeros((2, LANES), F32).at[0, :DN_HEADS].set(a_log).at[1, :DN_HEADS].set(dt_bias)
    blk = lambda j: (lambda gi, ci: (gi * n_chunk + ci, j))
    per_grp3 = lambda gi, ci: (gi, 0, 0)
    per_grp4 = lambda gi, ci: (gi, 0, 0, 0)
    fixed2 = lambda gi, ci: (0, 0)
    in_specs = [pl.BlockSpec((rows, DN_QKV), blk(0)),
                pl.BlockSpec((rows, D_MODEL), blk(DN_QKV // D_MODEL)),
                pl.BlockSpec((rows, LANES), blk((DN_QKV + D_MODEL) // LANES)),
                pl.BlockSpec((None, 2 * DN_HEADS, rows), lambda gi, ci: (gi * n_chunk + ci, 0, 0))]
    args = [proj, proj, proj, abt]
    if has_state:
        in_specs += [pl.BlockSpec((nseq, past, DN_QKV), per_grp3),
                     pl.BlockSpec((nseq, DN_HEADS, DN_DK, DN_DK), per_grp4)]
        args += [conv_state, s0]
    in_specs += [pl.BlockSpec((DN_CONV, 1, DN_QKV), lambda gi, ci: (0, 0, 0)),
                 pl.BlockSpec((2, LANES), fixed2),
                 pl.BlockSpec((1, LANES), fixed2)]
    args += [conv_w.reshape(DN_CONV, 1, DN_QKV), hp, o_norm.reshape(1, LANES)]
    n_state = n_grp * nseq
    return pl.pallas_call(
        functools.partial(_delta_kernel, seq_len=seq_len, has_state=has_state),
        grid=(n_grp, n_chunk),
        in_specs=in_specs,
        out_specs=[pl.BlockSpec((rows, D_MODEL), blk(0)),
                   pl.BlockSpec((nseq, past, DN_QKV), per_grp3),
                   pl.BlockSpec((nseq, DN_HEADS, DN_DK, DN_DK), per_grp4)],
        out_shape=[jax.ShapeDtypeStruct((n, D_MODEL), BF16),
                   jax.ShapeDtypeStruct((n_state, past, DN_QKV), F32),
                   jax.ShapeDtypeStruct((n_state, DN_HEADS, DN_DK, DN_DK), F32)],
        scratch_shapes=[pltpu.VMEM((nseq, DN_HALO + seq_len, DN_QKV), F32)],
        compiler_params=_params(2),
        name="gated_delta",
    )(*args)


def kernel(x_prompt, x_sample, cache_k, cache_v, page_table, state_conv, state_delta_conv, state_delta_S, norm_gain, attn_w_in, attn_q_norm, attn_k_norm, attn_lambda_q, attn_lambda_k, attn_subln, attn_w_out, conv_w_in, conv_dw_w, conv_dw_b, conv_ln_g, conv_ln_b, conv_w_out, delta_w_in, delta_conv_w, delta_a_log, delta_dt_bias, delta_o_norm, delta_w_out):
    bp, tp, _ = x_prompt.shape
    bs, ts, _ = x_sample.shape
    past_len = page_table.shape[1] * PAGE_SIZE
    yp = x_prompt.reshape(bp * tp, D_MODEL)
    ys = x_sample.reshape(bs * ts, D_MODEL)

    cos_p, sin_p = _rope_tables(jnp.arange(tp))
    cos_s, sin_s = _rope_tables(past_len + jnp.arange(ts))
    cos_s = jnp.tile(cos_s, (ROW_TILE // ts, 1))
    sin_s = jnp.tile(sin_s, (ROW_TILE // ts, 1))

    kp_l, vp_l, ks_l, vs_l = [], [], [], []
    cp_l, cs_l = [], []
    dcp_l, dcs_l, dsp_l, dss_l = [], [], [], []
    for i in range(DEPTH):
        kind = i % N_MIXERS
        j = i // N_MIXERS
        gain = norm_gain[i]
        if kind == 0:
            lam_init = 0.8 - 0.6 * math.exp(-0.3 * i)
            w_in = attn_w_in[j].astype(BF16)
            w_out = attn_w_out[j].astype(BF16)
            lq, lk = attn_lambda_q[j], attn_lambda_k[j]
            qb, k, v, z, kb, vb = _attn_proj(yp, gain, w_in, attn_q_norm[j], attn_k_norm[j], cos_p, sin_p, BF16)
            gp = _flash_attention(qb, kb, vb, z, lq, lk, attn_subln[j], lam_init, bp, tp)
            kp_l.append(k.reshape(bp, tp, ATT_HEADS, 2, ATT_DH))
            vp_l.append(v.reshape(bp, tp, ATT_HEADS, ATT_VD))
            yp = _out_proj(gp, w_out, yp)
            q, k, v, z, _, _ = _attn_proj(ys, gain, w_in, attn_q_norm[j], attn_k_norm[j], cos_s, sin_s, F32)
            gs = _sample_attention(q, k, v, z, cache_k, cache_v, j, page_table, lq, lk, attn_subln[j], lam_init)
            ks_l.append(k.reshape(bs, ts, ATT_HEADS, 2, ATT_DH))
            vs_l.append(v.reshape(bs, ts, ATT_HEADS, ATT_VD))
            ys = _out_proj(gs, w_out, ys)
        elif kind == 1:
            w_in = conv_w_in[j].astype(BF16)
            w_out = conv_w_out[j].astype(BF16)
            tail = (conv_dw_w[j], conv_dw_b[j], conv_ln_g[j], conv_ln_b[j], w_out)
            proj = _norm_proj(yp, gain, w_in).reshape(bp, tp, 3 * D_MODEL)
            y3, buf = _conformer(proj, yp.reshape(bp, tp, D_MODEL), None, *tail, nseq=1, tt=CONV_TILE)
            yp = y3.reshape(bp * tp, D_MODEL)
            cp_l.append(buf)
            proj = _norm_proj(ys, gain, w_in).reshape(bs, ts, 3 * D_MODEL)
            y3, buf = _conformer(proj, ys.reshape(bs, ts, D_MODEL), state_conv[j], *tail,
                                 nseq=DN_ROWS // ts, tt=ts)
            ys = y3.reshape(bs * ts, D_MODEL)
            cs_l.append(buf)
        else:
            n_in = delta_w_in.shape[2]
            n_pad = -n_in % LANES
            w_in = jnp.pad(delta_w_in[j], ((0, 0), (0, n_pad))).astype(BF16)
            w_out = delta_w_out[j].astype(BF16)
            tail = (delta_conv_w[j], delta_a_log[j], delta_dt_bias[j], delta_o_norm[j])
            proj = _norm_proj(yp, gain, w_in)
            g, buf, s_new = _gated_delta(proj, None, None, *tail, bsz=bp, t=tp, seq_len=DN_ROWS)
            yp = _out_proj(g, w_out, yp)
            dcp_l.append(buf)
            dsp_l.append(s_new)
            proj = _norm_proj(ys, gain, w_in)
            g, buf, s_new = _gated_delta(proj, state_delta_conv[j], state_delta_S[j], *tail,
                                         bsz=bs, t=ts, seq_len=ts)
            ys = _out_proj(g, w_out, ys)
            dcs_l.append(buf)
            dss_l.append(s_new)
    return (yp.reshape(bp, tp, D_MODEL), ys.reshape(bs, ts, D_MODEL),
            jnp.stack(kp_l), jnp.stack(vp_l), jnp.stack(ks_l), jnp.stack(vs_l),
            jnp.stack(cp_l), jnp.stack(cs_l), jnp.stack(dcp_l), jnp.stack(dcs_l),
            jnp.stack(dsp_l), jnp.stack(dss_l))
```

```python
import functools
import math

import jax
import jax.numpy as jnp
from jax import lax
from jax.experimental import pallas as pl
from jax.experimental.pallas import tpu as pltpu

F32 = jnp.float32
BF16 = jnp.bfloat16

D_MODEL = 1024
DEPTH = 4
N_MIXERS = 3
PAGE_SIZE = 128
ATT_HEADS = 8
ATT_DH = 64
ATT_VD = 128
ROPE_THETA = 10000.0
NEG_INF = -1e30
CONV_WIDTH = 31
DN_HEADS = 8
DN_DK = 128
DN_QKV = 3072
DN_CONV = 4
EPS = 1e-6

LANES = 128
ROW_TILE = 256
NORM_PROJ_TILE = 512
OUT_PROJ_TILE = 1024
ATT_BLOCK = 512
PAGES_PER_STEP = 8
CONV_TILE = 256
CONV_HALO = 32
DN_ROWS = 128
DN_HALO = 8
VMEM_LIMIT = 56 * 1024 * 1024


def _params(n_axes):
    return pltpu.CompilerParams(dimension_semantics=("arbitrary",) * n_axes,
                                vmem_limit_bytes=VMEM_LIMIT)


def _mm(a, b):
    return jnp.dot(a.astype(BF16), b.astype(BF16), preferred_element_type=F32)


def _mm_nt(a, b):
    return lax.dot_general(a.astype(BF16), b.astype(BF16), (((1,), (1,)), ((), ())),
                           preferred_element_type=F32)


def _mm_tn(a, b):
    return lax.dot_general(a.astype(BF16), b.astype(BF16), (((0,), (0,)), ((), ())),
                           preferred_element_type=F32)


def _sigmoid(x):
    return 1.0 / (1.0 + jnp.exp(-x))


def _silu(x):
    return x * _sigmoid(x)


def _softplus(x):
    return jnp.maximum(x, 0.0) + jnp.log(1.0 + jnp.exp(-jnp.abs(x)))


def _rms_rows(x):
    return x * lax.rsqrt(jnp.mean(x * x, axis=-1, keepdims=True) + EPS)


def _norm_proj_kernel(x_ref, g_ref, w_ref, o_ref, *, col_chunk):
    h = (_rms_rows(x_ref[...]) * g_ref[...]).astype(BF16)
    n_out = o_ref.shape[1]
    for c0 in range(0, n_out, col_chunk):
        c1 = min(c0 + col_chunk, n_out)
        o_ref[:, c0:c1] = jnp.dot(h, w_ref[:, c0:c1], preferred_element_type=F32)


def _norm_proj(x, gain, w_bf16):
    n, n_out = x.shape[0], w_bf16.shape[1]
    tile = min(NORM_PROJ_TILE, n)
    return pl.pallas_call(
        functools.partial(_norm_proj_kernel, col_chunk=1024),
        grid=(n // tile,),
        in_specs=[pl.BlockSpec((tile, D_MODEL), lambda i: (i, 0)),
                  pl.BlockSpec((1, D_MODEL), lambda i: (0, 0)),
                  pl.BlockSpec((D_MODEL, n_out), lambda i: (0, 0))],
        out_specs=pl.BlockSpec((tile, n_out), lambda i: (i, 0)),
        out_shape=jax.ShapeDtypeStruct((n, n_out), F32),
        compiler_params=_params(1),
        name="norm_proj",
    )(x, gain.reshape(1, D_MODEL), w_bf16)


def _attn_proj_kernel(*refs, n_passthrough):
    x_ref, g_ref, w_ref, qn_ref, kn_ref, cos_ref, sin_ref = refs[:7]
    q_ref, k_ref, v_ref, z_ref, kb_ref, vb_ref = refs[7 + n_passthrough:]
    h = (_rms_rows(x_ref[...]) * g_ref[...]).astype(BF16)
    width = ATT_HEADS * ATT_VD
    r = lax.broadcasted_iota(jnp.int32, (LANES, LANES), 0)
    c = lax.broadcasted_iota(jnp.int32, (LANES, LANES), 1)
    group_ones = ((r >> 6) == (c >> 6)).astype(BF16)
    lane = lax.broadcasted_iota(jnp.int32, (1, LANES), 1)
    first_half = (lane & (ATT_DH - 1)) < (ATT_DH // 2)
    cos = cos_ref[...]
    sin = sin_ref[...]

    def norm_rope(raw, gain):
        ms = jnp.dot((raw * raw).astype(BF16), group_ones, preferred_element_type=F32) * (1.0 / ATT_DH)
        xn = raw * lax.rsqrt(ms + EPS) * gain
        rot = jnp.where(first_half, pltpu.roll(xn, LANES - ATT_DH // 2, 1), pltpu.roll(xn, ATT_DH // 2, 1))
        return xn * cos + rot * sin

    pair = 2 * LANES
    for c0 in range(0, width, pair):
        q_raw = jnp.dot(h, w_ref[:, c0:c0 + pair], preferred_element_type=F32)
        k_raw = jnp.dot(h, w_ref[:, width + c0:width + c0 + pair], preferred_element_type=F32)
        for s in range(0, pair, LANES):
            sl = slice(c0 + s, c0 + s + LANES)
            q = norm_rope(q_raw[:, s:s + LANES], qn_ref[...]) * (ATT_DH ** -0.5)
            q_ref[:, sl] = q.astype(q_ref.dtype)
            k = norm_rope(k_raw[:, s:s + LANES], kn_ref[...])
            k_ref[:, sl] = k
            kb_ref[:, sl] = k.astype(BF16)
    v = jnp.dot(h, w_ref[:, 2 * width:3 * width], preferred_element_type=F32)
    v_ref[...] = v
    vb_ref[...] = v.astype(BF16)
    z_ref[...] = jnp.dot(h, w_ref[:, 3 * width:4 * width], preferred_element_type=F32)


def _attn_proj(x, gain, w_bf16, qn, kn, cos, sin, q_dtype, slot, n_slot, kv_stack):
    n = x.shape[0]
    n_tab = cos.shape[0] // ROW_TILE
    width = ATT_HEADS * ATT_VD
    row = lambda i: (i, 0)
    fixed = lambda i: (0, 0)
    out = lambda dt: jax.ShapeDtypeStruct((n, width), dt)
    stacked = jax.ShapeDtypeStruct((n_slot, n, width), F32)
    stacked_spec = pl.BlockSpec((None, ROW_TILE, width), lambda i: (slot, i, 0))
    in_specs = [pl.BlockSpec((ROW_TILE, D_MODEL), row),
                pl.BlockSpec((1, D_MODEL), fixed),
                pl.BlockSpec((D_MODEL, 4 * width), fixed),
                pl.BlockSpec((1, LANES), fixed),
                pl.BlockSpec((1, LANES), fixed),
                pl.BlockSpec((ROW_TILE, LANES), lambda i: (i % n_tab, 0)),
                pl.BlockSpec((ROW_TILE, LANES), lambda i: (i % n_tab, 0))]
    args = [x, gain.reshape(1, D_MODEL), w_bf16, jnp.tile(qn, 2).reshape(1, LANES),
            jnp.tile(kn, 2).reshape(1, LANES), cos, sin]
    aliases = {}
    if kv_stack is not None:
        aliases = {len(args): 1, len(args) + 1: 2}
        in_specs += [pl.BlockSpec(memory_space=pl.ANY)] * 2
        args += list(kv_stack)
    return pl.pallas_call(
        functools.partial(_attn_proj_kernel, n_passthrough=len(aliases)),
        grid=(n // ROW_TILE,),
        in_specs=in_specs,
        out_specs=[pl.BlockSpec((ROW_TILE, width), row), stacked_spec, stacked_spec]
                  + [pl.BlockSpec((ROW_TILE, width), row)] * 3,
        out_shape=[out(q_dtype), stacked, stacked, out(F32), out(BF16), out(BF16)],
        input_output_aliases=aliases,
        compiler_params=_params(1),
        name="attn_proj",
    )(*args)


def _rope_tables(pos):
    inv = ROPE_THETA ** (-jnp.arange(0, ATT_DH, 2, dtype=F32) / ATT_DH)
    ang = pos.astype(F32)[:, None] * inv[None, :]
    cos = jnp.cos(ang)
    sin = jnp.sin(ang)
    cos = jnp.concatenate([cos, cos, cos, cos], axis=-1)
    sin = jnp.concatenate([-sin, sin, -sin, sin], axis=-1)
    return cos, sin


def _diff_lambda_in_kernel(lq_ref, lk_ref, lam_init):
    e = jnp.exp(jnp.sum(lq_ref[...] * lk_ref[...], axis=1, keepdims=True))
    return e[0:1] - e[1:2] + lam_init


def _diff_finish(acc1, l1, acc2, l2, lam, lam_init, subln, z):
    o = acc1 / l1 - lam * (acc2 / l2)
    o = _rms_rows(o) * subln * (1.0 - lam_init)
    return o * _silu(z)


def _flash_kernel(q_ref, k_ref, v_ref, z_ref, lq_ref, lk_ref, sub_ref, o_ref, *, lam_init):
    blk = ATT_BLOCK
    nblk = q_ref.shape[0] // blk
    lam = _diff_lambda_in_kernel(lq_ref, lk_ref, lam_init)
    lane = lax.broadcasted_iota(jnp.int32, (1, LANES), 1)
    kv_i = lax.broadcasted_iota(jnp.int32, (blk, 2 * blk), 0)
    q_i = lax.broadcasted_iota(jnp.int32, (blk, 2 * blk), 1) & (blk - 1)
    diag_visible = q_i >= kv_i
    q_maps = {}

    def scores(qi, ki):
        if qi not in q_maps:
            q = q_ref[qi * blk:(qi + 1) * blk, :]
            zero = jnp.zeros_like(q)
            q_maps[qi] = jnp.concatenate([jnp.where(lane < ATT_DH, q, zero), jnp.where(lane >= ATT_DH, q, zero)],
                                         axis=0)
        return lax.dot_general(k_ref[ki * blk:(ki + 1) * blk, :], q_maps[qi], (((1,), (1,)), ((), ())),
                               preferred_element_type=F32)

    pairs = [(qi, ki) for qi in range(nblk) for ki in range(qi + 1)]
    s_next = scores(*pairs[0])
    for n, (qi, ki) in enumerate(pairs):
        s = s_next
        if n + 1 < len(pairs):
            s_next = scores(*pairs[n + 1])
        if ki == 0:
            m = jnp.full((1, 2 * blk), NEG_INF, F32)
            l = jnp.zeros((1, 2 * blk), F32)
            acc = jnp.zeros((ATT_VD, 2 * blk), F32)
        if ki == qi:
            s = jnp.where(diag_visible, s, NEG_INF)
        m_new = jnp.maximum(m, jnp.max(s, axis=0, keepdims=True))
        alpha = jnp.exp(m - m_new)
        p = jnp.exp(s - m_new)
        l = alpha * l + jnp.sum(p, axis=0, keepdims=True)
        pv = lax.dot_general(v_ref[ki * blk:(ki + 1) * blk, :], p.astype(BF16), (((0,), (0,)), ((), ())),
                             preferred_element_type=F32)
        acc = alpha * acc + pv
        m = m_new
        if ki == qi:
            rows = slice(qi * blk, (qi + 1) * blk)
            o_t = acc[:, :blk] / l[:, :blk] - lam * (acc[:, blk:] / l[:, blk:])
            o_t = (o_t * lax.rsqrt(jnp.mean(o_t * o_t, axis=0, keepdims=True) + EPS) * sub_ref[...]
                   * (1.0 - lam_init))
            o_ref[rows, :] = (o_t.T * _silu(z_ref[rows, :])).astype(o_ref.dtype)


def _flash_attention(qb, kb, vb, z, lq, lk, subln, lam_init, batch, seq):
    n = batch * seq
    per_head = lambda b, h: (b, h)
    fixed = lambda b, h: (0, 0)
    return pl.pallas_call(
        functools.partial(_flash_kernel, lam_init=lam_init),
        grid=(batch, ATT_HEADS),
        in_specs=[pl.BlockSpec((seq, LANES), per_head),
                  pl.BlockSpec((seq, LANES), per_head),
                  pl.BlockSpec((seq, LANES), per_head),
                  pl.BlockSpec((seq, LANES), per_head),
                  pl.BlockSpec((2, ATT_DH), fixed),
                  pl.BlockSpec((2, ATT_DH), fixed),
                  pl.BlockSpec((ATT_VD, 1), fixed)],
        out_specs=pl.BlockSpec((seq, LANES), per_head),
        out_shape=jax.ShapeDtypeStruct((n, ATT_HEADS * ATT_VD), BF16),
        compiler_params=_params(2),
        name="flash_diff_attention",
    )(qb, kb, vb, z, lq, lk, subln.reshape(ATT_VD, 1))


def _sample_attn_kernel(*refs, lam_init, n_steps):
    g = PAGES_PER_STEP
    q_ref = refs[1]
    kt_refs = refs[2:2 + g]
    v_refs = refs[2 + g:2 + 2 * g]
    kn_ref, vn_ref, z_ref, lq_ref, lk_ref, sub_ref, o_ref, qbd_sc, m_sc, l_sc, acc_sc = refs[2 + 2 * g:]
    p = pl.program_id(1)
    t = q_ref.shape[0]
    width = q_ref.shape[1]
    n_rows = 2 * ATT_HEADS * t
    per_head = 2 * t

    @pl.when(p == 0)
    def _():
        q = jnp.concatenate([q_ref[...]] * (2 * ATT_HEADS), axis=0)
        r = lax.broadcasted_iota(jnp.int32, (n_rows, width), 0)
        c = lax.broadcasted_iota(jnp.int32, (n_rows, width), 1)
        qbd_sc[...] = jnp.where((r >> 3) == (c >> 6), q, 0.0).astype(BF16)
        m_sc[...] = jnp.full(m_sc.shape, NEG_INF, F32)
        l_sc[...] = jnp.zeros(l_sc.shape, F32)
        acc_sc[...] = jnp.zeros(acc_sc.shape, F32)

    def update(s, v_of_head):
        m_old = m_sc[...]
        m_new = jnp.maximum(m_old, jnp.max(s, axis=1, keepdims=True))
        alpha = jnp.exp(m_old - m_new)
        e = jnp.exp(s - m_new)
        l_sc[...] = alpha * l_sc[...] + jnp.sum(e, axis=1, keepdims=True)
        m_sc[...] = m_new
        e = e.astype(BF16)
        for h in range(ATT_HEADS):
            rows = slice(h * per_head, (h + 1) * per_head)
            acc_sc[rows, :] = alpha[rows] * acc_sc[rows, :] + jnp.dot(e[rows], v_of_head(h),
                                                                      preferred_element_type=F32)

    halves = [slice(0, g // 2), slice(g // 2, g)]
    scores = []
    for half in halves:
        kt = jnp.concatenate([r[...].astype(BF16) for r in kt_refs[half]], axis=1)
        scores.append(jnp.dot(qbd_sc[...], kt, preferred_element_type=F32))
    for half, s in zip(halves, scores):
        def cached_v(h, refs_=v_refs[half]):
            return jnp.concatenate([r[pl.ds(h, PAGE_SIZE, stride=ATT_HEADS), :].astype(BF16) for r in refs_],
                                   axis=0)

        update(s, cached_v)

    @pl.when(p == n_steps - 1)
    def _():
        pad = n_rows - t
        kn = jnp.concatenate([kn_ref[...], jnp.zeros((pad, width), F32)], axis=0).astype(BF16)
        s = lax.dot_general(qbd_sc[...], kn, (((1,), (1,)), ((), ())), preferred_element_type=F32)
        row = lax.broadcasted_iota(jnp.int32, (n_rows, n_rows), 0)
        col = lax.broadcasted_iota(jnp.int32, (n_rows, n_rows), 1)
        s = jnp.where(col <= (row & (t - 1)), s, NEG_INF)

        def new_v(h):
            vh = vn_ref[:, h * ATT_VD:(h + 1) * ATT_VD]
            return jnp.concatenate([vh, jnp.zeros((pad, ATT_VD), F32)], axis=0).astype(BF16)

        update(s, new_v)
        lam = _diff_lambda_in_kernel(lq_ref, lk_ref, lam_init)
        acc = acc_sc[...]
        l = l_sc[...]
        for h in range(ATT_HEADS):
            r0 = h * per_head
            cols = slice(h * ATT_VD, (h + 1) * ATT_VD)
            o_ref[:, cols] = _diff_finish(acc[r0:r0 + t], l[r0:r0 + t], acc[r0 + t:r0 + 2 * t],
                                          l[r0 + t:r0 + 2 * t], lam, lam_init, sub_ref[...], z_ref[:, cols])


def _sample_attention(q, k_new, v_new, z, cache_k, cache_v, layer, page_table, lq, lk, subln, lam_init):
    batch, n_pages = page_table.shape
    t = q.shape[0] // batch
    h, dh, vd = ATT_HEADS, ATT_DH, ATT_VD
    width = h * vd
    g = PAGES_PER_STEP
    n_steps = n_pages // g
    assert t == 8 and n_pages % g == 0 and cache_k.shape[2] == PAGE_SIZE
    n_layer, n_pool = cache_k.shape[:2]
    ckt = cache_k.transpose(0, 1, 3, 4, 5, 2).reshape(n_layer, n_pool, width, PAGE_SIZE)
    cv = cache_v.reshape(n_layer, n_pool, PAGE_SIZE * h, vd)
    pt = page_table.reshape(-1)

    def page(i):
        return lambda b, p, pt_ref: (layer, pt_ref[b * n_pages + p * g + i], 0, 0)

    per_b = lambda b, p, pt_ref: (b, 0)
    fixed = lambda b, p, pt_ref: (0, 0)
    grid_spec = pltpu.PrefetchScalarGridSpec(
        num_scalar_prefetch=1,
        grid=(batch, n_steps),
        in_specs=([pl.BlockSpec((t, width), per_b)]
                  + [pl.BlockSpec((None, None, width, PAGE_SIZE), page(i)) for i in range(g)]
                  + [pl.BlockSpec((None, None, PAGE_SIZE * h, vd), page(i)) for i in range(g)]
                  + [pl.BlockSpec((t, width), per_b),
                     pl.BlockSpec((t, width), per_b),
                     pl.BlockSpec((t, width), per_b),
                     pl.BlockSpec((2, dh), fixed),
                     pl.BlockSpec((2, dh), fixed),
                     pl.BlockSpec((1, vd), fixed)]),
        out_specs=pl.BlockSpec((t, width), per_b),
        scratch_shapes=[pltpu.VMEM((2 * h * t, width), BF16),
                        pltpu.VMEM((2 * h * t, 1), F32),
                        pltpu.VMEM((2 * h * t, 1), F32),
                        pltpu.VMEM((2 * h * t, vd), F32)],
    )
    return pl.pallas_call(
        functools.partial(_sample_attn_kernel, lam_init=lam_init, n_steps=n_steps),
        grid_spec=grid_spec,
        out_shape=jax.ShapeDtypeStruct((batch * t, width), F32),
        compiler_params=_params(2),
        name="paged_diff_attention",
    )(pt, q, *([ckt] * g), *([cv] * g), k_new, v_new, z, lq, lk, subln.reshape(1, vd))


def _out_proj_kernel(g_ref, w_ref, x_ref, y_ref):
    y_ref[...] = x_ref[...] + jnp.dot(g_ref[...].astype(BF16), w_ref[...], preferred_element_type=F32)


def _out_proj(g, w_bf16, x):
    n = x.shape[0]
    tile = min(OUT_PROJ_TILE, n)
    row = lambda i: (i, 0)
    return pl.pallas_call(
        _out_proj_kernel,
        grid=(n // tile,),
        in_specs=[pl.BlockSpec((tile, g.shape[1]), row),
                  pl.BlockSpec(w_bf16.shape, lambda i: (0, 0)),
                  pl.BlockSpec((tile, D_MODEL), row)],
        out_specs=pl.BlockSpec((tile, D_MODEL), row),
        out_shape=jax.ShapeDtypeStruct((n, D_MODEL), F32),
        compiler_params=_params(1),
        name="out_proj",
    )(g, w_bf16, x)


def _conformer_kernel(*refs, has_state):
    if has_state:
        (a_ref, b_ref, z_ref, x_ref, st_ref, dw_ref, db_ref, lg_ref, lb_ref, w_ref,
         y_ref, nb_ref, ubuf) = refs
    else:
        (a_ref, b_ref, z_ref, x_ref, dw_ref, db_ref, lg_ref, lb_ref, w_ref,
         y_ref, nb_ref, ubuf) = refs
    ti = pl.program_id(1)
    nseq, tt, ch = a_ref.shape
    past = CONV_WIDTH - 1
    lead = CONV_HALO - past

    @pl.when(ti == 0)
    def _():
        ubuf[:, 0:CONV_HALO, :] = jnp.zeros((nseq, CONV_HALO, ch), F32)
        if has_state:
            ubuf[:, lead:CONV_HALO, :] = st_ref[...]

    @pl.when(ti > 0)
    def _():
        ubuf[:, 0:CONV_HALO, :] = ubuf[:, tt:tt + CONV_HALO, :]

    ubuf[:, CONV_HALO:CONV_HALO + tt, :] = a_ref[...] * _sigmoid(b_ref[...])
    sub = 8
    acc = jnp.zeros((nseq, tt, ch), F32) + db_ref[...]
    for s in range(sub):
        taps = [j for j in range(CONV_WIDTH) if (lead + j) % sub == s]
        part = None
        n_win = tt + (sub if s else 0)
        for j in taps:
            base = lead + j - s
            term = dw_ref[j] * ubuf[:, base:base + n_win, :]
            part = term if part is None else part + term
        acc = acc + part[:, s:s + tt, :]
    mean = jnp.mean(acc, axis=-1, keepdims=True)
    cen = acc - mean
    var = jnp.mean(cen * cen, axis=-1, keepdims=True)
    c = _silu(cen * lax.rsqrt(var + EPS) * lg_ref[...] + lb_ref[...])
    gated = (c * _silu(z_ref[...])).reshape(nseq * tt, ch).astype(BF16)
    y = jnp.dot(gated, w_ref[...], preferred_element_type=F32)
    y_ref[...] = x_ref[...] + y.reshape(nseq, tt, ch)

    @pl.when(ti == pl.num_programs(1) - 1)
    def _():
        nb_ref[...] = ubuf[:, tt + lead:tt + CONV_HALO, :]


def _conformer(proj, x, state, dw_w, dw_b, ln_g, ln_b, w_out_bf16, nseq, tt):
    bsz, t, _ = x.shape
    ch = D_MODEL
    past = CONV_WIDTH - 1
    col = lambda j: (lambda bi, ti: (bi, ti, j))
    fixed2 = lambda bi, ti: (0, 0)
    in_specs = [pl.BlockSpec((nseq, tt, ch), col(0)),
                pl.BlockSpec((nseq, tt, ch), col(1)),
                pl.BlockSpec((nseq, tt, ch), col(2)),
                pl.BlockSpec((nseq, tt, ch), col(0))]
    args = [proj, proj, proj, x]
    if state is not None:
        in_specs.append(pl.BlockSpec((nseq, past, ch), lambda bi, ti: (bi, 0, 0)))
        args.append(state)
    in_specs += [pl.BlockSpec((CONV_WIDTH, 1, ch), lambda bi, ti: (0, 0, 0)),
                 pl.BlockSpec((1, ch), fixed2),
                 pl.BlockSpec((1, ch), fixed2),
                 pl.BlockSpec((1, ch), fixed2),
                 pl.BlockSpec((ch, D_MODEL), fixed2)]
    args += [dw_w.reshape(CONV_WIDTH, 1, ch), dw_b.reshape(1, ch), ln_g.reshape(1, ch), ln_b.reshape(1, ch),
             w_out_bf16]
    return pl.pallas_call(
        functools.partial(_conformer_kernel, has_state=state is not None),
        grid=(bsz // nseq, t // tt),
        in_specs=in_specs,
        out_specs=[pl.BlockSpec((nseq, tt, D_MODEL), col(0)),
                   pl.BlockSpec((nseq, past, ch), lambda bi, ti: (bi, 0, 0))],
        out_shape=[jax.ShapeDtypeStruct((bsz, t, D_MODEL), F32),
                   jax.ShapeDtypeStruct((bsz, past, ch), F32)],
        scratch_shapes=[pltpu.VMEM((nseq, CONV_HALO + tt, ch), F32)],
        compiler_params=_params(2),
        name="conformer_conv",
    )(*args)


def _unit_lower_inverses(ms, eye, n_stage):
    n = eye.shape[0]
    dot = functools.partial(jnp.dot, preferred_element_type=F32)
    powers = [-m for m in ms]
    invs = [eye + p for p in powers]
    powers = [_mm(p, p) for p in powers]
    for _ in range(n_stage - 2):
        both = [_mm(jnp.concatenate([p, inv], axis=0), p) for p, inv in zip(powers, invs)]
        invs = [inv + b[n:] for inv, b in zip(invs, both)]
        powers = [b[:n] for b in both]
    invs = [inv + _mm(inv, p) for inv, p in zip(invs, powers)]
    mh = [m.astype(BF16) for m in ms]
    ml = [(m - h.astype(F32)).astype(BF16) for m, h in zip(ms, mh)]
    ih = [inv.astype(BF16) for inv in invs]
    il = [(inv - h.astype(F32)).astype(BF16) for inv, h in zip(invs, ih)]
    hi = [dot(jnp.concatenate([a, b], axis=0), c) for a, b, c in zip(mh, ml, ih)]
    lo = [dot(a, c) for a, c in zip(mh, il)]
    resid = [eye - inv - (h[:n] + h[n:] + l) for inv, h, l in zip(invs, hi, lo)]
    return [inv + _mm(inv, r) for inv, r in zip(invs, resid)]


def _delta_kernel(*refs, seq_len, has_state):
    if has_state:
        (x_ref, z_ref, ab_ref, abt_ref, cst_ref, s0_ref, cw_ref, hp_ref, on_ref,
         g_ref, nc_ref, s_ref, cbuf) = refs
    else:
        (x_ref, z_ref, ab_ref, abt_ref, cw_ref, hp_ref, on_ref,
         g_ref, nc_ref, s_ref, cbuf) = refs
    ci = pl.program_id(1)
    rows, width = x_ref.shape
    nseq = rows // seq_len
    past = DN_CONV - 1
    lead = DN_HALO - past
    dk = DN_DK

    @pl.when(ci == 0)
    def _():
        cbuf[:, 0:DN_HALO, :] = jnp.zeros((nseq, DN_HALO, width), F32)
        if has_state:
            cbuf[:, lead:DN_HALO, :] = cst_ref[...]
            s_ref[...] = s0_ref[...]
        else:
            s_ref[...] = jnp.zeros(s_ref.shape, F32)

    @pl.when(ci > 0)
    def _():
        cbuf[:, 0:DN_HALO, :] = cbuf[:, seq_len:seq_len + DN_HALO, :]

    cbuf[:, DN_HALO:DN_HALO + seq_len, :] = x_ref[...].reshape(nseq, seq_len, width)
    conv = jnp.zeros((nseq, seq_len, width), F32)
    for j in range(DN_CONV):
        conv = conv + cw_ref[j] * cbuf[:, lead + j:lead + j + seq_len, :]
    qkv = _silu(conv).reshape(rows, width)

    @pl.when(ci == pl.num_programs(1) - 1)
    def _():
        nc_ref[...] = cbuf[:, seq_len + lead:seq_len + DN_HALO, :]

    pair = 2 * rows
    ri = lax.broadcasted_iota(jnp.int32, (pair, pair), 0)
    cj = lax.broadcasted_iota(jnp.int32, (pair, pair), 1)
    shift = int(math.log2(seq_len))
    head_shift = int(math.log2(rows))
    same = (ri >> shift) == (cj >> shift)
    lower = (ri >= cj) & same
    strict = (ri > cj) & same
    upper = (ri <= cj) & same
    eye = (ri == cj).astype(F32)
    ab = ab_ref[...]
    abt = abt_ref[...]
    hp = hp_ref[...]

    def stack(fn, heads):
        return jnp.concatenate([fn(h) for h in heads], axis=0)

    def block_diag(x):
        r = lax.broadcasted_iota(jnp.int32, (x.shape[0], 2 * dk), 0)
        c = lax.broadcasted_iota(jnp.int32, (x.shape[0], 2 * dk), 1)
        own = ((r >> head_shift) & 1) == (c >> int(math.log2(dk)))
        return jnp.where(own, jnp.concatenate([x, x], axis=1), 0.0)

    def seq_of(n):
        r = lax.broadcasted_iota(jnp.int32, (n, 1), 0)
        return (r & (rows - 1)) >> shift

    first_heads = list(range(0, DN_HEADS, 2))

    def prepare(h0):
        heads = (h0, h0 + 1)
        q = stack(lambda h: qkv[:, h * dk:(h + 1) * dk], heads)
        k = stack(lambda h: qkv[:, (DN_HEADS + h) * dk:(DN_HEADS + h + 1) * dk], heads)
        v = stack(lambda h: qkv[:, (2 * DN_HEADS + h) * dk:(2 * DN_HEADS + h + 1) * dk], heads)
        q = q * lax.rsqrt(jnp.sum(q * q, axis=-1, keepdims=True) + EPS) * (dk ** -0.5)
        k = k * lax.rsqrt(jnp.sum(k * k, axis=-1, keepdims=True) + EPS)
        neg_a = lambda h: -jnp.exp(hp[0:1, h:h + 1])
        dt_bias = lambda h: hp[1:2, h:h + 1]
        g_col = stack(lambda h: neg_a(h) * _softplus(ab[:, h:h + 1] + dt_bias(h)), heads)
        g_row = jnp.concatenate([neg_a(h) * _softplus(abt[h:h + 1, :] + dt_bias(h)) for h in heads], axis=1)
        beta = stack(lambda h: _sigmoid(ab[:, DN_HEADS + h:DN_HEADS + h + 1]), heads)
        gc_col = jnp.sum(jnp.where(lower, g_row, 0.0), axis=1, keepdims=True)
        gc_row = jnp.sum(jnp.where(upper, g_col, 0.0), axis=0, keepdims=True)
        g_tot = jnp.sum(jnp.where(same, g_row, 0.0), axis=1, keepdims=True)
        decay = jnp.where(lower, jnp.exp(jnp.where(lower, gc_col - gc_row, 0.0)), 0.0)
        kb = k * beta
        kbq = jnp.concatenate([kb, q], axis=0)
        rhs = jnp.concatenate([v * beta, kb * jnp.exp(gc_col)], axis=1)
        qg = q * jnp.exp(gc_col)
        kd = block_diag(k * jnp.exp(g_tot - gc_col))
        return dict(k=k, kbq=kbq, rhs=rhs, qg=qg, kd=kd, decay=decay, e_tot=jnp.exp(g_tot))

    pre = [prepare(h0) for h0 in first_heads]
    kk_qk = [_mm_nt(p["kbq"], p["k"]) for p in pre]
    ms = [jnp.where(strict, x[:pair] * p["decay"], 0.0) for x, p in zip(kk_qk, pre)]
    qks = [x[pair:] * p["decay"] for x, p in zip(kk_qk, pre)]
    tinvs = _unit_lower_inverses(ms, eye, shift)
    u_ws = [_mm(t, p["rhs"]) for t, p in zip(tinvs, pre)]
    us = [x[:, :dk] for x in u_ws]
    w_qgs = [block_diag(jnp.concatenate([x[:, dk:], p["qg"]], axis=0)) for x, p in zip(u_ws, pre)]
    kds = [p["kd"] for p in pre]
    e_tots = [p["e_tot"] for p in pre]

    def state_rows(e_of_head):
        return jnp.concatenate([jnp.broadcast_to(e_of_head(i), (dk, 1)) for i in range(2)], axis=0)

    def load_state(s, h0):
        return s_ref[s, h0:h0 + 2].reshape(2 * dk, dk)

    if nseq == 1:
        s_olds = [load_state(0, h0) for h0 in first_heads]
        ws_qs = [_mm(a, s_old) for a, s_old in zip(w_qgs, s_olds)]
        v_news = [u - x[:pair] for u, x in zip(us, ws_qs)]
        os_ = [x[pair:] + _mm(qk, v_new) for x, qk, v_new in zip(ws_qs, qks, v_news)]
        for h0, s_old, e_tot, kd, v_new in zip(first_heads, s_olds, e_tots, kds, v_news):
            scale = state_rows(lambda i: e_tot[i * rows:i * rows + 1])
            s_ref[0, h0:h0 + 2] = (s_old * scale + _mm_tn(kd, v_new)).reshape(2, dk, dk)
    else:
        seq2 = seq_of(2 * pair)
        seq1 = seq_of(pair)
        head1 = lax.broadcasted_iota(jnp.int32, (pair, 1), 0) >> head_shift

        def read_state(s, accs):
            return tuple(acc + _mm(jnp.where(seq2 == s, a, 0.0), load_state(s, h0))
                         for acc, a, h0 in zip(accs, w_qgs, first_heads))

        ws_qs = lax.fori_loop(0, nseq, read_state,
                              tuple(jnp.zeros((2 * pair, dk), F32) for _ in first_heads))
        v_news = [u - x[:pair] for u, x in zip(us, ws_qs)]
        os_ = [x[pair:] + _mm(qk, v_new) for x, qk, v_new in zip(ws_qs, qks, v_news)]

        def write_state(s, carry):
            mine = seq1 == s
            for h0, e_tot, kd, v_new in zip(first_heads, e_tots, kds, v_news):
                scale = state_rows(
                    lambda i: jnp.max(jnp.where(mine & (head1 == i), e_tot, 0.0), axis=0, keepdims=True))
                s_new = load_state(s, h0) * scale + _mm_tn(jnp.where(mine, kd, 0.0), v_new)
                s_ref[s, h0:h0 + 2] = s_new.reshape(2, dk, dk)
            return carry

        lax.fori_loop(0, nseq, write_state, 0)

    for h0, o in zip(first_heads, os_):
        o = _rms_rows(o) * on_ref[...]
        for i in range(2):
            cols = slice((h0 + i) * dk, (h0 + i + 1) * dk)
            g_ref[:, cols] = (o[i * rows:(i + 1) * rows] * _silu(z_ref[:, cols])).astype(g_ref.dtype)


def _gated_delta(proj, conv_state, s0, conv_w, a_log, dt_bias, o_norm, bsz, t, seq_len):
    n = bsz * t
    rows = DN_ROWS
    nseq = rows // seq_len
    n_chunk = max(t // rows, 1)
    n_grp = n // (rows * n_chunk)
    past = DN_CONV - 1
    has_state = conv_state is not None
    ab = proj[:, DN_QKV + D_MODEL:DN_QKV + D_MODEL + 2 * DN_HEADS]
    abt = ab.reshape(n // rows, rows, 2 * DN_HEADS).transpose(0, 2, 1)
    hp = jnp.zeros((2, LANES), F32).at[0, :DN_HEADS].set(a_log).at[1, :DN_HEADS].set(dt_bias)
    blk = lambda j: (lambda gi, ci: (gi * n_chunk + ci, j))
    per_grp3 = lambda gi, ci: (gi, 0, 0)
    per_grp4 = lambda gi, ci: (gi, 0, 0, 0)
    fixed2 = lambda gi, ci: (0, 0)
    in_specs = [pl.BlockSpec((rows, DN_QKV), blk(0)),
                pl.BlockSpec((rows, D_MODEL), blk(DN_QKV // D_MODEL)),
                pl.BlockSpec((rows, LANES), blk((DN_QKV + D_MODEL) // LANES)),
                pl.BlockSpec((None, 2 * DN_HEADS, rows), lambda gi, ci: (gi * n_chunk + ci, 0, 0))]
    args = [proj, proj, proj, abt]
    if has_state:
        in_specs += [pl.BlockSpec((nseq, past, DN_QKV), per_grp3),
                     pl.BlockSpec((nseq, DN_HEADS, DN_DK, DN_DK), per_grp4)]
        args += [conv_state, s0]
    in_specs += [pl.BlockSpec((DN_CONV, 1, DN_QKV), lambda gi, ci: (0, 0, 0)),
                 pl.BlockSpec((2, LANES), fixed2),
                 pl.BlockSpec((1, LANES), fixed2)]
    args += [conv_w.reshape(DN_CONV, 1, DN_QKV), hp, o_norm.reshape(1, LANES)]
    n_state = n_grp * nseq
    return pl.pallas_call(
        functools.partial(_delta_kernel, seq_len=seq_len, has_state=has_state),
        grid=(n_grp, n_chunk),
        in_specs=in_specs,
        out_specs=[pl.BlockSpec((rows, D_MODEL), blk(0)),
                   pl.BlockSpec((nseq, past, DN_QKV), per_grp3),
                   pl.BlockSpec((nseq, DN_HEADS, DN_DK, DN_DK), per_grp4)],
        out_shape=[jax.ShapeDtypeStruct((n, D_MODEL), BF16),
                   jax.ShapeDtypeStruct((n_state, past, DN_QKV), F32),
                   jax.ShapeDtypeStruct((n_state, DN_HEADS, DN_DK, DN_DK), F32)],
        scratch_shapes=[pltpu.VMEM((nseq, DN_HALO + seq_len, DN_QKV), F32)],
        compiler_params=_params(2),
        name="gated_delta",
    )(*args)


def kernel(x_prompt, x_sample, cache_k, cache_v, page_table, state_conv, state_delta_conv, state_delta_S, norm_gain, attn_w_in, attn_q_norm, attn_k_norm, attn_lambda_q, attn_lambda_k, attn_subln, attn_w_out, conv_w_in, conv_dw_w, conv_dw_b, conv_ln_g, conv_ln_b, conv_w_out, delta_w_in, delta_conv_w, delta_a_log, delta_dt_bias, delta_o_norm, delta_w_out):
    bp, tp, _ = x_prompt.shape
    bs, ts, _ = x_sample.shape
    past_len = page_table.shape[1] * PAGE_SIZE
    yp = x_prompt.reshape(bp * tp, D_MODEL)
    ys = x_sample.reshape(bs * ts, D_MODEL)

    cos_p, sin_p = _rope_tables(jnp.arange(tp))
    cos_s, sin_s = _rope_tables(past_len + jnp.arange(ts))
    cos_s = jnp.tile(cos_s, (ROW_TILE // ts, 1))
    sin_s = jnp.tile(sin_s, (ROW_TILE // ts, 1))

    n_attn = attn_w_in.shape[0]
    kv_p = kv_s = None
    cp_l, cs_l = [], []
    dcp_l, dcs_l, dsp_l, dss_l = [], [], [], []
    for i in range(DEPTH):
        kind = i % N_MIXERS
        j = i // N_MIXERS
        gain = norm_gain[i]
        if kind == 0:
            lam_init = 0.8 - 0.6 * math.exp(-0.3 * i)
            w_in = attn_w_in[j].astype(BF16)
            w_out = attn_w_out[j].astype(BF16)
            lq, lk = attn_lambda_q[j], attn_lambda_k[j]
            qb, k, v, z, kb, vb = _attn_proj(yp, gain, w_in, attn_q_norm[j], attn_k_norm[j], cos_p, sin_p, BF16,
                                             j, n_attn, kv_p)
            kv_p = (k, v)
            gp = _flash_attention(qb, kb, vb, z, lq, lk, attn_subln[j], lam_init, bp, tp)
            yp = _out_proj(gp, w_out, yp)
            q, k, v, z, _, _ = _attn_proj(ys, gain, w_in, attn_q_norm[j], attn_k_norm[j], cos_s, sin_s, F32,
                                          j, n_attn, kv_s)
            kv_s = (k, v)
            gs = _sample_attention(q, k[j], v[j], z, cache_k, cache_v, j, page_table, lq, lk, attn_subln[j],
                                   lam_init)
            ys = _out_proj(gs, w_out, ys)
        elif kind == 1:
            w_in = conv_w_in[j].astype(BF16)
            w_out = conv_w_out[j].astype(BF16)
            tail = (conv_dw_w[j], conv_dw_b[j], conv_ln_g[j], conv_ln_b[j], w_out)
            proj = _norm_proj(yp, gain, w_in).reshape(bp, tp, 3 * D_MODEL)
            y3, buf = _conformer(proj, yp.reshape(bp, tp, D_MODEL), None, *tail, nseq=1, tt=CONV_TILE)
            yp = y3.reshape(bp * tp, D_MODEL)
            cp_l.append(buf)
            proj = _norm_proj(ys, gain, w_in).reshape(bs, ts, 3 * D_MODEL)
            y3, buf = _conformer(proj, ys.reshape(bs, ts, D_MODEL), state_conv[j], *tail,
                                 nseq=DN_ROWS // ts, tt=ts)
            ys = y3.reshape(bs * ts, D_MODEL)
            cs_l.append(buf)
        else:
            n_in = delta_w_in.shape[2]
            n_pad = -n_in % LANES
            w_in = jnp.pad(delta_w_in[j], ((0, 0), (0, n_pad))).astype(BF16)
            w_out = delta_w_out[j].astype(BF16)
            tail = (delta_conv_w[j], delta_a_log[j], delta_dt_bias[j], delta_o_norm[j])
            proj = _norm_proj(yp, gain, w_in)
            g, buf, s_new = _gated_delta(proj, None, None, *tail, bsz=bp, t=tp, seq_len=DN_ROWS)
            yp = _out_proj(g, w_out, yp)
            dcp_l.append(buf)
            dsp_l.append(s_new)
            proj = _norm_proj(ys, gain, w_in)
            g, buf, s_new = _gated_delta(proj, state_delta_conv[j], state_delta_S[j], *tail,
                                         bsz=bs, t=ts, seq_len=ts)
            ys = _out_proj(g, w_out, ys)
            dcs_l.append(buf)
            dss_l.append(s_new)
    return (yp.reshape(bp, tp, D_MODEL), ys.reshape(bs, ts, D_MODEL),
            kv_p[0].reshape(n_attn, bp, tp, ATT_HEADS, 2, ATT_DH), kv_p[1].reshape(n_attn, bp, tp, ATT_HEADS, ATT_VD),
            kv_s[0].reshape(n_attn, bs, ts, ATT_HEADS, 2, ATT_DH), kv_s[1].reshape(n_attn, bs, ts, ATT_HEADS, ATT_VD),
            jnp.stack(cp_l), jnp.stack(cs_l), jnp.stack(dcp_l), jnp.stack(dcs_l),
            jnp.stack(dsp_l), jnp.stack(dss_l))
```

```python
import functools
import math

import jax
import jax.numpy as jnp
from jax import lax
from jax.experimental import pallas as pl
from jax.experimental.pallas import tpu as pltpu

F32 = jnp.float32
BF16 = jnp.bfloat16

D_MODEL = 1024
DEPTH = 4
N_MIXERS = 3
PAGE_SIZE = 128
ATT_HEADS = 8
ATT_DH = 64
ATT_VD = 128
ROPE_THETA = 10000.0
NEG_INF = -1e30
CONV_WIDTH = 31
DN_HEADS = 8
DN_DK = 128
DN_QKV = 3072
DN_CONV = 4
EPS = 1e-6

LANES = 128
ROW_TILE = 256
NORM_PROJ_TILE = 512
OUT_PROJ_TILE = 1024
ATT_BLOCK = 512
PAGES_PER_STEP = 16
CONV_TILE = 256
CONV_HALO = 32
DN_ROWS = 128
DN_HALO = 8
VMEM_LIMIT = 56 * 1024 * 1024


def _params(n_axes):
    return pltpu.CompilerParams(dimension_semantics=("arbitrary",) * n_axes,
                                vmem_limit_bytes=VMEM_LIMIT)


def _mm(a, b):
    return jnp.dot(a.astype(BF16), b.astype(BF16), preferred_element_type=F32)


def _mm_nt(a, b):
    return lax.dot_general(a.astype(BF16), b.astype(BF16), (((1,), (1,)), ((), ())),
                           preferred_element_type=F32)


def _mm_tn(a, b):
    return lax.dot_general(a.astype(BF16), b.astype(BF16), (((0,), (0,)), ((), ())),
                           preferred_element_type=F32)


def _sigmoid(x):
    return 1.0 / (1.0 + jnp.exp(-x))


def _silu(x):
    return x * _sigmoid(x)


def _softplus(x):
    return jnp.maximum(x, 0.0) + jnp.log(1.0 + jnp.exp(-jnp.abs(x)))


def _rms_rows(x):
    return x * lax.rsqrt(jnp.mean(x * x, axis=-1, keepdims=True) + EPS)


def _norm_proj_kernel(x_ref, g_ref, w_ref, o_ref, *, col_chunk):
    h = (_rms_rows(x_ref[...]) * g_ref[...]).astype(BF16)
    n_out = o_ref.shape[1]
    for c0 in range(0, n_out, col_chunk):
        c1 = min(c0 + col_chunk, n_out)
        o_ref[:, c0:c1] = jnp.dot(h, w_ref[:, c0:c1], preferred_element_type=F32)


def _norm_proj(x, gain, w_bf16):
    n, n_out = x.shape[0], w_bf16.shape[1]
    tile = min(NORM_PROJ_TILE, n)
    return pl.pallas_call(
        functools.partial(_norm_proj_kernel, col_chunk=1024),
        grid=(n // tile,),
        in_specs=[pl.BlockSpec((tile, D_MODEL), lambda i: (i, 0)),
                  pl.BlockSpec((1, D_MODEL), lambda i: (0, 0)),
                  pl.BlockSpec((D_MODEL, n_out), lambda i: (0, 0))],
        out_specs=pl.BlockSpec((tile, n_out), lambda i: (i, 0)),
        out_shape=jax.ShapeDtypeStruct((n, n_out), F32),
        compiler_params=_params(1),
        name="norm_proj",
    )(x, gain.reshape(1, D_MODEL), w_bf16)


def _attn_proj_kernel(x_ref, g_ref, w_ref, qn_ref, kn_ref, cos_ref, sin_ref,
                      q_ref, k_ref, v_ref, z_ref, kb_ref, vb_ref):
    h = (_rms_rows(x_ref[...]) * g_ref[...]).astype(BF16)
    width = ATT_HEADS * ATT_VD
    r = lax.broadcasted_iota(jnp.int32, (LANES, LANES), 0)
    c = lax.broadcasted_iota(jnp.int32, (LANES, LANES), 1)
    group_ones = ((r >> 6) == (c >> 6)).astype(BF16)
    lane = lax.broadcasted_iota(jnp.int32, (1, LANES), 1)
    first_half = (lane & (ATT_DH - 1)) < (ATT_DH // 2)
    cos = cos_ref[...]
    sin = sin_ref[...]

    def norm_rope(raw, gain):
        ms = jnp.dot((raw * raw).astype(BF16), group_ones, preferred_element_type=F32) * (1.0 / ATT_DH)
        xn = raw * lax.rsqrt(ms + EPS) * gain
        rot = jnp.where(first_half, pltpu.roll(xn, LANES - ATT_DH // 2, 1), pltpu.roll(xn, ATT_DH // 2, 1))
        return xn * cos + rot * sin

    pair = 2 * LANES
    for c0 in range(0, width, pair):
        q_raw = jnp.dot(h, w_ref[:, c0:c0 + pair], preferred_element_type=F32)
        k_raw = jnp.dot(h, w_ref[:, width + c0:width + c0 + pair], preferred_element_type=F32)
        for s in range(0, pair, LANES):
            sl = slice(c0 + s, c0 + s + LANES)
            q = norm_rope(q_raw[:, s:s + LANES], qn_ref[...]) * (ATT_DH ** -0.5)
            q_ref[:, sl] = q.astype(q_ref.dtype)
            k = norm_rope(k_raw[:, s:s + LANES], kn_ref[...])
            k_ref[:, sl] = k
            kb_ref[:, sl] = k.astype(BF16)
    v = jnp.dot(h, w_ref[:, 2 * width:3 * width], preferred_element_type=F32)
    v_ref[...] = v
    vb_ref[...] = v.astype(BF16)
    z_ref[...] = jnp.dot(h, w_ref[:, 3 * width:4 * width], preferred_element_type=F32)


def _attn_proj(x, gain, w_bf16, qn, kn, cos, sin, q_dtype):
    n = x.shape[0]
    n_tab = cos.shape[0] // ROW_TILE
    width = ATT_HEADS * ATT_VD
    row = lambda i: (i, 0)
    fixed = lambda i: (0, 0)
    out = lambda dt: jax.ShapeDtypeStruct((n, width), dt)
    return pl.pallas_call(
        _attn_proj_kernel,
        grid=(n // ROW_TILE,),
        in_specs=[pl.BlockSpec((ROW_TILE, D_MODEL), row),
                  pl.BlockSpec((1, D_MODEL), fixed),
                  pl.BlockSpec((D_MODEL, 4 * width), fixed),
                  pl.BlockSpec((1, LANES), fixed),
                  pl.BlockSpec((1, LANES), fixed),
                  pl.BlockSpec((ROW_TILE, LANES), lambda i: (i % n_tab, 0)),
                  pl.BlockSpec((ROW_TILE, LANES), lambda i: (i % n_tab, 0))],
        out_specs=[pl.BlockSpec((ROW_TILE, width), row)] * 6,
        out_shape=[out(q_dtype), out(F32), out(F32), out(F32), out(BF16), out(BF16)],
        compiler_params=_params(1),
        name="attn_proj",
    )(x, gain.reshape(1, D_MODEL), w_bf16, jnp.tile(qn, 2).reshape(1, LANES), jnp.tile(kn, 2).reshape(1, LANES),
      cos, sin)


def _rope_tables(pos):
    inv = ROPE_THETA ** (-jnp.arange(0, ATT_DH, 2, dtype=F32) / ATT_DH)
    ang = pos.astype(F32)[:, None] * inv[None, :]
    cos = jnp.cos(ang)
    sin = jnp.sin(ang)
    cos = jnp.concatenate([cos, cos, cos, cos], axis=-1)
    sin = jnp.concatenate([-sin, sin, -sin, sin], axis=-1)
    return cos, sin


def _diff_lambda_in_kernel(lq_ref, lk_ref, lam_init):
    e = jnp.exp(jnp.sum(lq_ref[...] * lk_ref[...], axis=1, keepdims=True))
    return e[0:1] - e[1:2] + lam_init


def _diff_finish(acc1, l1, acc2, l2, lam, lam_init, subln, z):
    o = acc1 / l1 - lam * (acc2 / l2)
    o = _rms_rows(o) * subln * (1.0 - lam_init)
    return o * _silu(z)


def _flash_kernel(q_ref, k_ref, v_ref, z_ref, lq_ref, lk_ref, sub_ref, o_ref, *, lam_init):
    blk = ATT_BLOCK
    nblk = q_ref.shape[0] // blk
    lam = _diff_lambda_in_kernel(lq_ref, lk_ref, lam_init)
    lane = lax.broadcasted_iota(jnp.int32, (1, LANES), 1)
    kv_i = lax.broadcasted_iota(jnp.int32, (blk, 2 * blk), 0)
    q_i = lax.broadcasted_iota(jnp.int32, (blk, 2 * blk), 1) & (blk - 1)
    diag_visible = q_i >= kv_i
    q_maps = {}

    def scores(qi, ki):
        if qi not in q_maps:
            q = q_ref[qi * blk:(qi + 1) * blk, :]
            zero = jnp.zeros_like(q)
            q_maps[qi] = jnp.concatenate([jnp.where(lane < ATT_DH, q, zero), jnp.where(lane >= ATT_DH, q, zero)],
                                         axis=0)
        return lax.dot_general(k_ref[ki * blk:(ki + 1) * blk, :], q_maps[qi], (((1,), (1,)), ((), ())),
                               preferred_element_type=F32)

    pairs = [(qi, ki) for qi in range(nblk) for ki in range(qi + 1)]
    s_next = scores(*pairs[0])
    for n, (qi, ki) in enumerate(pairs):
        s = s_next
        if n + 1 < len(pairs):
            s_next = scores(*pairs[n + 1])
        if ki == 0:
            m = jnp.full((1, 2 * blk), NEG_INF, F32)
            l = jnp.zeros((1, 2 * blk), F32)
            acc = jnp.zeros((ATT_VD, 2 * blk), F32)
        if ki == qi:
            s = jnp.where(diag_visible, s, NEG_INF)
        m_new = jnp.maximum(m, jnp.max(s, axis=0, keepdims=True))
        alpha = jnp.exp(m - m_new)
        p = jnp.exp(s - m_new)
        l = alpha * l + jnp.sum(p, axis=0, keepdims=True)
        pv = lax.dot_general(v_ref[ki * blk:(ki + 1) * blk, :], p.astype(BF16), (((0,), (0,)), ((), ())),
                             preferred_element_type=F32)
        acc = alpha * acc + pv
        m = m_new
        if ki == qi:
            rows = slice(qi * blk, (qi + 1) * blk)
            o_t = acc[:, :blk] / l[:, :blk] - lam * (acc[:, blk:] / l[:, blk:])
            o_t = (o_t * lax.rsqrt(jnp.mean(o_t * o_t, axis=0, keepdims=True) + EPS) * sub_ref[...]
                   * (1.0 - lam_init))
            o_ref[rows, :] = (o_t.T * _silu(z_ref[rows, :])).astype(o_ref.dtype)


def _flash_attention(qb, kb, vb, z, lq, lk, subln, lam_init, batch, seq):
    n = batch * seq
    per_head = lambda b, h: (b, h)
    fixed = lambda b, h: (0, 0)
    return pl.pallas_call(
        functools.partial(_flash_kernel, lam_init=lam_init),
        grid=(batch, ATT_HEADS),
        in_specs=[pl.BlockSpec((seq, LANES), per_head),
                  pl.BlockSpec((seq, LANES), per_head),
                  pl.BlockSpec((seq, LANES), per_head),
                  pl.BlockSpec((seq, LANES), per_head),
                  pl.BlockSpec((2, ATT_DH), fixed),
                  pl.BlockSpec((2, ATT_DH), fixed),
                  pl.BlockSpec((ATT_VD, 1), fixed)],
        out_specs=pl.BlockSpec((seq, LANES), per_head),
        out_shape=jax.ShapeDtypeStruct((n, ATT_HEADS * ATT_VD), BF16),
        compiler_params=_params(2),
        name="flash_diff_attention",
    )(qb, kb, vb, z, lq, lk, subln.reshape(ATT_VD, 1))


def _sample_attn_kernel(*refs, lam_init, n_steps):
    g = PAGES_PER_STEP
    q_ref = refs[1]
    kt_refs = refs[2:2 + g]
    v_refs = refs[2 + g:2 + 2 * g]
    kn_ref, vn_ref, z_ref, lq_ref, lk_ref, sub_ref, o_ref, qbd_sc, m_sc, l_sc, acc_sc = refs[2 + 2 * g:]
    p = pl.program_id(1)
    t = q_ref.shape[0]
    width = q_ref.shape[1]
    n_rows = 2 * ATT_HEADS * t
    per_head = 2 * t

    @pl.when(p == 0)
    def _():
        q = jnp.concatenate([q_ref[...]] * (2 * ATT_HEADS), axis=0)
        r = lax.broadcasted_iota(jnp.int32, (n_rows, width), 0)
        c = lax.broadcasted_iota(jnp.int32, (n_rows, width), 1)
        qbd_sc[...] = jnp.where((r >> 3) == (c >> 6), q, 0.0).astype(BF16)
        m_sc[...] = jnp.full(m_sc.shape, NEG_INF, F32)
        l_sc[...] = jnp.zeros(l_sc.shape, F32)
        acc_sc[...] = jnp.zeros(acc_sc.shape, F32)

    def update(s, v_of_head):
        m_old = m_sc[...]
        m_new = jnp.maximum(m_old, jnp.max(s, axis=1, keepdims=True))
        alpha = jnp.exp(m_old - m_new)
        e = jnp.exp(s - m_new)
        l_sc[...] = alpha * l_sc[...] + jnp.sum(e, axis=1, keepdims=True)
        m_sc[...] = m_new
        e = e.astype(BF16)
        for h in range(ATT_HEADS):
            rows = slice(h * per_head, (h + 1) * per_head)
            acc_sc[rows, :] = alpha[rows] * acc_sc[rows, :] + jnp.dot(e[rows], v_of_head(h),
                                                                      preferred_element_type=F32)

    halves = [slice(0, g // 2), slice(g // 2, g)]
    scores = []
    for half in halves:
        kt = jnp.concatenate([r[...].astype(BF16) for r in kt_refs[half]], axis=1)
        scores.append(jnp.dot(qbd_sc[...], kt, preferred_element_type=F32))
    for half, s in zip(halves, scores):
        def cached_v(h, refs_=v_refs[half]):
            return jnp.concatenate([r[pl.ds(h, PAGE_SIZE, stride=ATT_HEADS), :].astype(BF16) for r in refs_],
                                   axis=0)

        update(s, cached_v)

    @pl.when(p == n_steps - 1)
    def _():
        pad = n_rows - t
        kn = jnp.concatenate([kn_ref[...], jnp.zeros((pad, width), F32)], axis=0).astype(BF16)
        s = lax.dot_general(qbd_sc[...], kn, (((1,), (1,)), ((), ())), preferred_element_type=F32)
        row = lax.broadcasted_iota(jnp.int32, (n_rows, n_rows), 0)
        col = lax.broadcasted_iota(jnp.int32, (n_rows, n_rows), 1)
        s = jnp.where(col <= (row & (t - 1)), s, NEG_INF)

        def new_v(h):
            vh = vn_ref[:, h * ATT_VD:(h + 1) * ATT_VD]
            return jnp.concatenate([vh, jnp.zeros((pad, ATT_VD), F32)], axis=0).astype(BF16)

        update(s, new_v)
        lam = _diff_lambda_in_kernel(lq_ref, lk_ref, lam_init)
        acc = acc_sc[...]
        l = l_sc[...]
        for h in range(ATT_HEADS):
            r0 = h * per_head
            cols = slice(h * ATT_VD, (h + 1) * ATT_VD)
            o_ref[:, cols] = _diff_finish(acc[r0:r0 + t], l[r0:r0 + t], acc[r0 + t:r0 + 2 * t],
                                          l[r0 + t:r0 + 2 * t], lam, lam_init, sub_ref[...], z_ref[:, cols])


def _sample_attention(q, k_new, v_new, z, cache_k, cache_v, layer, page_table, lq, lk, subln, lam_init):
    batch, n_pages = page_table.shape
    t = q.shape[0] // batch
    h, dh, vd = ATT_HEADS, ATT_DH, ATT_VD
    width = h * vd
    g = PAGES_PER_STEP
    n_steps = n_pages // g
    assert t == 8 and n_pages % g == 0 and cache_k.shape[2] == PAGE_SIZE
    n_layer, n_pool = cache_k.shape[:2]
    ckt = cache_k.transpose(0, 1, 3, 4, 5, 2).reshape(n_layer, n_pool, width, PAGE_SIZE)
    cv = cache_v.reshape(n_layer, n_pool, PAGE_SIZE * h, vd)
    pt = page_table.reshape(-1)

    def page(i):
        return lambda b, p, pt_ref: (layer, pt_ref[b * n_pages + p * g + i], 0, 0)

    per_b = lambda b, p, pt_ref: (b, 0)
    fixed = lambda b, p, pt_ref: (0, 0)
    grid_spec = pltpu.PrefetchScalarGridSpec(
        num_scalar_prefetch=1,
        grid=(batch, n_steps),
        in_specs=([pl.BlockSpec((t, width), per_b)]
                  + [pl.BlockSpec((None, None, width, PAGE_SIZE), page(i)) for i in range(g)]
                  + [pl.BlockSpec((None, None, PAGE_SIZE * h, vd), page(i)) for i in range(g)]
                  + [pl.BlockSpec((t, width), per_b),
                     pl.BlockSpec((t, width), per_b),
                     pl.BlockSpec((t, width), per_b),
                     pl.BlockSpec((2, dh), fixed),
                     pl.BlockSpec((2, dh), fixed),
                     pl.BlockSpec((1, vd), fixed)]),
        out_specs=pl.BlockSpec((t, width), per_b),
        scratch_shapes=[pltpu.VMEM((2 * h * t, width), BF16),
                        pltpu.VMEM((2 * h * t, 1), F32),
                        pltpu.VMEM((2 * h * t, 1), F32),
                        pltpu.VMEM((2 * h * t, vd), F32)],
    )
    return pl.pallas_call(
        functools.partial(_sample_attn_kernel, lam_init=lam_init, n_steps=n_steps),
        grid_spec=grid_spec,
        out_shape=jax.ShapeDtypeStruct((batch * t, width), F32),
        compiler_params=_params(2),
        name="paged_diff_attention",
    )(pt, q, *([ckt] * g), *([cv] * g), k_new, v_new, z, lq, lk, subln.reshape(1, vd))


def _out_proj_kernel(g_ref, w_ref, x_ref, y_ref):
    y_ref[...] = x_ref[...] + jnp.dot(g_ref[...].astype(BF16), w_ref[...], preferred_element_type=F32)


def _out_proj(g, w_bf16, x):
    n = x.shape[0]
    tile = min(OUT_PROJ_TILE, n)
    row = lambda i: (i, 0)
    return pl.pallas_call(
        _out_proj_kernel,
        grid=(n // tile,),
        in_specs=[pl.BlockSpec((tile, g.shape[1]), row),
                  pl.BlockSpec(w_bf16.shape, lambda i: (0, 0)),
                  pl.BlockSpec((tile, D_MODEL), row)],
        out_specs=pl.BlockSpec((tile, D_MODEL), row),
        out_shape=jax.ShapeDtypeStruct((n, D_MODEL), F32),
        compiler_params=_params(1),
        name="out_proj",
    )(g, w_bf16, x)


def _conformer_kernel(*refs, has_state):
    if has_state:
        (a_ref, b_ref, z_ref, x_ref, st_ref, dw_ref, db_ref, lg_ref, lb_ref, w_ref,
         y_ref, nb_ref, ubuf) = refs
    else:
        (a_ref, b_ref, z_ref, x_ref, dw_ref, db_ref, lg_ref, lb_ref, w_ref,
         y_ref, nb_ref, ubuf) = refs
    ti = pl.program_id(1)
    nseq, tt, ch = a_ref.shape
    past = CONV_WIDTH - 1
    lead = CONV_HALO - past

    @pl.when(ti == 0)
    def _():
        ubuf[:, 0:CONV_HALO, :] = jnp.zeros((nseq, CONV_HALO, ch), F32)
        if has_state:
            ubuf[:, lead:CONV_HALO, :] = st_ref[...]

    @pl.when(ti > 0)
    def _():
        ubuf[:, 0:CONV_HALO, :] = ubuf[:, tt:tt + CONV_HALO, :]

    ubuf[:, CONV_HALO:CONV_HALO + tt, :] = a_ref[...] * _sigmoid(b_ref[...])
    sub = 8
    acc = jnp.zeros((nseq, tt, ch), F32) + db_ref[...]
    for s in range(sub):
        taps = [j for j in range(CONV_WIDTH) if (lead + j) % sub == s]
        part = None
        n_win = tt + (sub if s else 0)
        for j in taps:
            base = lead + j - s
            term = dw_ref[j] * ubuf[:, base:base + n_win, :]
            part = term if part is None else part + term
        acc = acc + part[:, s:s + tt, :]
    mean = jnp.mean(acc, axis=-1, keepdims=True)
    cen = acc - mean
    var = jnp.mean(cen * cen, axis=-1, keepdims=True)
    c = _silu(cen * lax.rsqrt(var + EPS) * lg_ref[...] + lb_ref[...])
    gated = (c * _silu(z_ref[...])).reshape(nseq * tt, ch).astype(BF16)
    y = jnp.dot(gated, w_ref[...], preferred_element_type=F32)
    y_ref[...] = x_ref[...] + y.reshape(nseq, tt, ch)

    @pl.when(ti == pl.num_programs(1) - 1)
    def _():
        nb_ref[...] = ubuf[:, tt + lead:tt + CONV_HALO, :]


def _conformer(proj, x, state, dw_w, dw_b, ln_g, ln_b, w_out_bf16, nseq, tt):
    bsz, t, _ = x.shape
    ch = D_MODEL
    past = CONV_WIDTH - 1
    col = lambda j: (lambda bi, ti: (bi, ti, j))
    fixed2 = lambda bi, ti: (0, 0)
    in_specs = [pl.BlockSpec((nseq, tt, ch), col(0)),
                pl.BlockSpec((nseq, tt, ch), col(1)),
                pl.BlockSpec((nseq, tt, ch), col(2)),
                pl.BlockSpec((nseq, tt, ch), col(0))]
    args = [proj, proj, proj, x]
    if state is not None:
        in_specs.append(pl.BlockSpec((nseq, past, ch), lambda bi, ti: (bi, 0, 0)))
        args.append(state)
    in_specs += [pl.BlockSpec((CONV_WIDTH, 1, ch), lambda bi, ti: (0, 0, 0)),
                 pl.BlockSpec((1, ch), fixed2),
                 pl.BlockSpec((1, ch), fixed2),
                 pl.BlockSpec((1, ch), fixed2),
                 pl.BlockSpec((ch, D_MODEL), fixed2)]
    args += [dw_w.reshape(CONV_WIDTH, 1, ch), dw_b.reshape(1, ch), ln_g.reshape(1, ch), ln_b.reshape(1, ch),
             w_out_bf16]
    return pl.pallas_call(
        functools.partial(_conformer_kernel, has_state=state is not None),
        grid=(bsz // nseq, t // tt),
        in_specs=in_specs,
        out_specs=[pl.BlockSpec((nseq, tt, D_MODEL), col(0)),
                   pl.BlockSpec((nseq, past, ch), lambda bi, ti: (bi, 0, 0))],
        out_shape=[jax.ShapeDtypeStruct((bsz, t, D_MODEL), F32),
                   jax.ShapeDtypeStruct((bsz, past, ch), F32)],
        scratch_shapes=[pltpu.VMEM((nseq, CONV_HALO + tt, ch), F32)],
        compiler_params=_params(2),
        name="conformer_conv",
    )(*args)


def _unit_lower_inverses(ms, eye, n_stage):
    n = eye.shape[0]
    dot = functools.partial(jnp.dot, preferred_element_type=F32)
    powers = [-m for m in ms]
    invs = [eye + p for p in powers]
    powers = [_mm(p, p) for p in powers]
    for _ in range(n_stage - 2):
        both = [_mm(jnp.concatenate([p, inv], axis=0), p) for p, inv in zip(powers, invs)]
        invs = [inv + b[n:] for inv, b in zip(invs, both)]
        powers = [b[:n] for b in both]
    invs = [inv + _mm(inv, p) for inv, p in zip(invs, powers)]
    mh = [m.astype(BF16) for m in ms]
    ml = [(m - h.astype(F32)).astype(BF16) for m, h in zip(ms, mh)]
    ih = [inv.astype(BF16) for inv in invs]
    il = [(inv - h.astype(F32)).astype(BF16) for inv, h in zip(invs, ih)]
    hi = [dot(jnp.concatenate([a, b], axis=0), c) for a, b, c in zip(mh, ml, ih)]
    lo = [dot(a, c) for a, c in zip(mh, il)]
    resid = [eye - inv - (h[:n] + h[n:] + l) for inv, h, l in zip(invs, hi, lo)]
    return [inv + _mm(inv, r) for inv, r in zip(invs, resid)]


def _delta_kernel(*refs, seq_len, has_state):
    if has_state:
        (x_ref, z_ref, ab_ref, abt_ref, cst_ref, s0_ref, cw_ref, hp_ref, on_ref,
         g_ref, nc_ref, s_ref, cbuf) = refs
    else:
        (x_ref, z_ref, ab_ref, abt_ref, cw_ref, hp_ref, on_ref,
         g_ref, nc_ref, s_ref, cbuf) = refs
    ci = pl.program_id(1)
    rows, width = x_ref.shape
    nseq = rows // seq_len
    past = DN_CONV - 1
    lead = DN_HALO - past
    dk = DN_DK

    @pl.when(ci == 0)
    def _():
        cbuf[:, 0:DN_HALO, :] = jnp.zeros((nseq, DN_HALO, width), F32)
        if has_state:
            cbuf[:, lead:DN_HALO, :] = cst_ref[...]
            s_ref[...] = s0_ref[...]
        else:
            s_ref[...] = jnp.zeros(s_ref.shape, F32)

    @pl.when(ci > 0)
    def _():
        cbuf[:, 0:DN_HALO, :] = cbuf[:, seq_len:seq_len + DN_HALO, :]

    cbuf[:, DN_HALO:DN_HALO + seq_len, :] = x_ref[...].reshape(nseq, seq_len, width)
    conv = jnp.zeros((nseq, seq_len, width), F32)
    for j in range(DN_CONV):
        conv = conv + cw_ref[j] * cbuf[:, lead + j:lead + j + seq_len, :]
    qkv = _silu(conv).reshape(rows, width)

    @pl.when(ci == pl.num_programs(1) - 1)
    def _():
        nc_ref[...] = cbuf[:, seq_len + lead:seq_len + DN_HALO, :]

    pair = 2 * rows
    ri = lax.broadcasted_iota(jnp.int32, (pair, pair), 0)
    cj = lax.broadcasted_iota(jnp.int32, (pair, pair), 1)
    shift = int(math.log2(seq_len))
    head_shift = int(math.log2(rows))
    same = (ri >> shift) == (cj >> shift)
    lower = (ri >= cj) & same
    strict = (ri > cj) & same
    upper = (ri <= cj) & same
    eye = (ri == cj).astype(F32)
    ab = ab_ref[...]
    abt = abt_ref[...]
    hp = hp_ref[...]

    def stack(fn, heads):
        return jnp.concatenate([fn(h) for h in heads], axis=0)

    def block_diag(x):
        r = lax.broadcasted_iota(jnp.int32, (x.shape[0], 2 * dk), 0)
        c = lax.broadcasted_iota(jnp.int32, (x.shape[0], 2 * dk), 1)
        own = ((r >> head_shift) & 1) == (c >> int(math.log2(dk)))
        return jnp.where(own, jnp.concatenate([x, x], axis=1), 0.0)

    def seq_of(n):
        r = lax.broadcasted_iota(jnp.int32, (n, 1), 0)
        return (r & (rows - 1)) >> shift

    first_heads = list(range(0, DN_HEADS, 2))

    def prepare(h0):
        heads = (h0, h0 + 1)
        q = stack(lambda h: qkv[:, h * dk:(h + 1) * dk], heads)
        k = stack(lambda h: qkv[:, (DN_HEADS + h) * dk:(DN_HEADS + h + 1) * dk], heads)
        v = stack(lambda h: qkv[:, (2 * DN_HEADS + h) * dk:(2 * DN_HEADS + h + 1) * dk], heads)
        q = q * lax.rsqrt(jnp.sum(q * q, axis=-1, keepdims=True) + EPS) * (dk ** -0.5)
        k = k * lax.rsqrt(jnp.sum(k * k, axis=-1, keepdims=True) + EPS)
        neg_a = lambda h: -jnp.exp(hp[0:1, h:h + 1])
        dt_bias = lambda h: hp[1:2, h:h + 1]
        g_col = stack(lambda h: neg_a(h) * _softplus(ab[:, h:h + 1] + dt_bias(h)), heads)
        g_row = jnp.concatenate([neg_a(h) * _softplus(abt[h:h + 1, :] + dt_bias(h)) for h in heads], axis=1)
        beta = stack(lambda h: _sigmoid(ab[:, DN_HEADS + h:DN_HEADS + h + 1]), heads)
        gc_col = jnp.sum(jnp.where(lower, g_row, 0.0), axis=1, keepdims=True)
        gc_row = jnp.sum(jnp.where(upper, g_col, 0.0), axis=0, keepdims=True)
        g_tot = jnp.sum(jnp.where(same, g_row, 0.0), axis=1, keepdims=True)
        decay = jnp.where(lower, jnp.exp(jnp.where(lower, gc_col - gc_row, 0.0)), 0.0)
        kb = k * beta
        kbq = jnp.concatenate([kb, q], axis=0)
        rhs = jnp.concatenate([v * beta, kb * jnp.exp(gc_col)], axis=1)
        qg = q * jnp.exp(gc_col)
        kd = block_diag(k * jnp.exp(g_tot - gc_col))
        return dict(k=k, kbq=kbq, rhs=rhs, qg=qg, kd=kd, decay=decay, e_tot=jnp.exp(g_tot))

    pre = [prepare(h0) for h0 in first_heads]
    kk_qk = [_mm_nt(p["kbq"], p["k"]) for p in pre]
    ms = [jnp.where(strict, x[:pair] * p["decay"], 0.0) for x, p in zip(kk_qk, pre)]
    qks = [x[pair:] * p["decay"] for x, p in zip(kk_qk, pre)]
    tinvs = _unit_lower_inverses(ms, eye, shift)
    u_ws = [_mm(t, p["rhs"]) for t, p in zip(tinvs, pre)]
    us = [x[:, :dk] for x in u_ws]
    w_qgs = [block_diag(jnp.concatenate([x[:, dk:], p["qg"]], axis=0)) for x, p in zip(u_ws, pre)]
    kds = [p["kd"] for p in pre]
    e_tots = [p["e_tot"] for p in pre]

    def state_rows(e_of_head):
        return jnp.concatenate([jnp.broadcast_to(e_of_head(i), (dk, 1)) for i in range(2)], axis=0)

    def load_state(s, h0):
        return s_ref[s, h0:h0 + 2].reshape(2 * dk, dk)

    if nseq == 1:
        s_olds = [load_state(0, h0) for h0 in first_heads]
        ws_qs = [_mm(a, s_old) for a, s_old in zip(w_qgs, s_olds)]
        v_news = [u - x[:pair] for u, x in zip(us, ws_qs)]
        os_ = [x[pair:] + _mm(qk, v_new) for x, qk, v_new in zip(ws_qs, qks, v_news)]
        for h0, s_old, e_tot, kd, v_new in zip(first_heads, s_olds, e_tots, kds, v_news):
            scale = state_rows(lambda i: e_tot[i * rows:i * rows + 1])
            s_ref[0, h0:h0 + 2] = (s_old * scale + _mm_tn(kd, v_new)).reshape(2, dk, dk)
    else:
        seq2 = seq_of(2 * pair)
        seq1 = seq_of(pair)
        head1 = lax.broadcasted_iota(jnp.int32, (pair, 1), 0) >> head_shift

        def read_state(s, accs):
            return tuple(acc + _mm(jnp.where(seq2 == s, a, 0.0), load_state(s, h0))
                         for acc, a, h0 in zip(accs, w_qgs, first_heads))

        ws_qs = lax.fori_loop(0, nseq, read_state,
                              tuple(jnp.zeros((2 * pair, dk), F32) for _ in first_heads))
        v_news = [u - x[:pair] for u, x in zip(us, ws_qs)]
        os_ = [x[pair:] + _mm(qk, v_new) for x, qk, v_new in zip(ws_qs, qks, v_news)]

        def write_state(s, carry):
            mine = seq1 == s
            for h0, e_tot, kd, v_new in zip(first_heads, e_tots, kds, v_news):
                scale = state_rows(
                    lambda i: jnp.max(jnp.where(mine & (head1 == i), e_tot, 0.0), axis=0, keepdims=True))
                s_new = load_state(s, h0) * scale + _mm_tn(jnp.where(mine, kd, 0.0), v_new)
                s_ref[s, h0:h0 + 2] = s_new.reshape(2, dk, dk)
            return carry

        lax.fori_loop(0, nseq, write_state, 0)

    for h0, o in zip(first_heads, os_):
        o = _rms_rows(o) * on_ref[...]
        for i in range(2):
            cols = slice((h0 + i) * dk, (h0 + i + 1) * dk)
            g_ref[:, cols] = (o[i * rows:(i + 1) * rows] * _silu(z_ref[:, cols])).astype(g_ref.dtype)


def _gated_delta(proj, conv_state, s0, conv_w, a_log, dt_bias, o_norm, bsz, t, seq_len):
    n = bsz * t
    rows = DN_ROWS
    nseq = rows // seq_len
    n_chunk = max(t // rows, 1)
    n_grp = n // (rows * n_chunk)
    past = DN_CONV - 1
    has_state = conv_state is not None
    ab = proj[:, DN_QKV + D_MODEL:DN_QKV + D_MODEL + 2 * DN_HEADS]
    abt = ab.reshape(n // rows, rows, 2 * DN_HEADS).transpose(0, 2, 1)
    hp = jnp.zeros((2, LANES), F32).at[0, :DN_HEADS].set(a_log).at[1, :DN_HEADS].set(dt_bias)
    blk = lambda j: (lambda gi, ci: (gi * n_chunk + ci, j))
    per_grp3 = lambda gi, ci: (gi, 0, 0)
    per_grp4 = lambda gi, ci: (gi, 0, 0, 0)
    fixed2 = lambda gi, ci: (0, 0)
    in_specs = [pl.BlockSpec((rows, DN_QKV), blk(0)),
                pl.BlockSpec((rows, D_MODEL), blk(DN_QKV // D_MODEL)),
                pl.BlockSpec((rows, LANES), blk((DN_QKV + D_MODEL) // LANES)),
                pl.BlockSpec((None, 2 * DN_HEADS, rows), lambda gi, ci: (gi * n_chunk + ci, 0, 0))]
    args = [proj, proj, proj, abt]
    if has_state:
        in_specs += [pl.BlockSpec((nseq, past, DN_QKV), per_grp3),
                     pl.BlockSpec((nseq, DN_HEADS, DN_DK, DN_DK), per_grp4)]
        args += [conv_state, s0]
    in_specs += [pl.BlockSpec((DN_CONV, 1, DN_QKV), lambda gi, ci: (0, 0, 0)),
                 pl.BlockSpec((2, LANES), fixed2),
                 pl.BlockSpec((1, LANES), fixed2)]
    args += [conv_w.reshape(DN_CONV, 1, DN_QKV), hp, o_norm.reshape(1, LANES)]
    n_state = n_grp * nseq
    return pl.pallas_call(
        functools.partial(_delta_kernel, seq_len=seq_len, has_state=has_state),
        grid=(n_grp, n_chunk),
        in_specs=in_specs,
        out_specs=[pl.BlockSpec((rows, D_MODEL), blk(0)),
                   pl.BlockSpec((nseq, past, DN_QKV), per_grp3),
                   pl.BlockSpec((nseq, DN_HEADS, DN_DK, DN_DK), per_grp4)],
        out_shape=[jax.ShapeDtypeStruct((n, D_MODEL), BF16),
                   jax.ShapeDtypeStruct((n_state, past, DN_QKV), F32),
                   jax.ShapeDtypeStruct((n_state, DN_HEADS, DN_DK, DN_DK), F32)],
        scratch_shapes=[pltpu.VMEM((nseq, DN_HALO + seq_len, DN_QKV), F32)],
        compiler_params=_params(2),
        name="gated_delta",
    )(*args)


def kernel(x_prompt, x_sample, cache_k, cache_v, page_table, state_conv, state_delta_conv, state_delta_S, norm_gain, attn_w_in, attn_q_norm, attn_k_norm, attn_lambda_q, attn_lambda_k, attn_subln, attn_w_out, conv_w_in, conv_dw_w, conv_dw_b, conv_ln_g, conv_ln_b, conv_w_out, delta_w_in, delta_conv_w, delta_a_log, delta_dt_bias, delta_o_norm, delta_w_out):
    bp, tp, _ = x_prompt.shape
    bs, ts, _ = x_sample.shape
    past_len = page_table.shape[1] * PAGE_SIZE
    yp = x_prompt.reshape(bp * tp, D_MODEL)
    ys = x_sample.reshape(bs * ts, D_MODEL)

    cos_p, sin_p = _rope_tables(jnp.arange(tp))
    cos_s, sin_s = _rope_tables(past_len + jnp.arange(ts))
    cos_s = jnp.tile(cos_s, (ROW_TILE // ts, 1))
    sin_s = jnp.tile(sin_s, (ROW_TILE // ts, 1))

    kp_l, vp_l, ks_l, vs_l = [], [], [], []
    cp_l, cs_l = [], []
    dcp_l, dcs_l, dsp_l, dss_l = [], [], [], []
    for i in range(DEPTH):
        kind = i % N_MIXERS
        j = i // N_MIXERS
        gain = norm_gain[i]
        if kind == 0:
            lam_init = 0.8 - 0.6 * math.exp(-0.3 * i)
            w_in = attn_w_in[j].astype(BF16)
            w_out = attn_w_out[j].astype(BF16)
            lq, lk = attn_lambda_q[j], attn_lambda_k[j]
            qb, k, v, z, kb, vb = _attn_proj(yp, gain, w_in, attn_q_norm[j], attn_k_norm[j], cos_p, sin_p, BF16)
            gp = _flash_attention(qb, kb, vb, z, lq, lk, attn_subln[j], lam_init, bp, tp)
            kp_l.append(k.reshape(bp, tp, ATT_HEADS, 2, ATT_DH))
            vp_l.append(v.reshape(bp, tp, ATT_HEADS, ATT_VD))
            yp = _out_proj(gp, w_out, yp)
            q, k, v, z, _, _ = _attn_proj(ys, gain, w_in, attn_q_norm[j], attn_k_norm[j], cos_s, sin_s, F32)
            gs = _sample_attention(q, k, v, z, cache_k, cache_v, j, page_table, lq, lk, attn_subln[j], lam_init)
            ks_l.append(k.reshape(bs, ts, ATT_HEADS, 2, ATT_DH))
            vs_l.append(v.reshape(bs, ts, ATT_HEADS, ATT_VD))
            ys = _out_proj(gs, w_out, ys)
        elif kind == 1:
            w_in = conv_w_in[j].astype(BF16)
            w_out = conv_w_out[j].astype(BF16)
            tail = (conv_dw_w[j], conv_dw_b[j], conv_ln_g[j], conv_ln_b[j], w_out)
            proj = _norm_proj(yp, gain, w_in).reshape(bp, tp, 3 * D_MODEL)
            y3, buf = _conformer(proj, yp.reshape(bp, tp, D_MODEL), None, *tail, nseq=1, tt=CONV_TILE)
            yp = y3.reshape(bp * tp, D_MODEL)
            cp_l.append(buf)
            proj = _norm_proj(ys, gain, w_in).reshape(bs, ts, 3 * D_MODEL)
            y3, buf = _conformer(proj, ys.reshape(bs, ts, D_MODEL), state_conv[j], *tail,
                                 nseq=DN_ROWS // ts, tt=ts)
            ys = y3.reshape(bs * ts, D_MODEL)
            cs_l.append(buf)
        else:
            n_in = delta_w_in.shape[2]
            n_pad = -n_in % LANES
            w_in = jnp.pad(delta_w_in[j], ((0, 0), (0, n_pad))).astype(BF16)
            w_out = delta_w_out[j].astype(BF16)
            tail = (delta_conv_w[j], delta_a_log[j], delta_dt_bias[j], delta_o_norm[j])
            proj = _norm_proj(yp, gain, w_in)
            g, buf, s_new = _gated_delta(proj, None, None, *tail, bsz=bp, t=tp, seq_len=DN_ROWS)
            yp = _out_proj(g, w_out, yp)
            dcp_l.append(buf)
            dsp_l.append(s_new)
            proj = _norm_proj(ys, gain, w_in)
            g, buf, s_new = _gated_delta(proj, state_delta_conv[j], state_delta_S[j], *tail,
                                         bsz=bs, t=ts, seq_len=ts)
            ys = _out_proj(g, w_out, ys)
            dcs_l.append(buf)
            dss_l.append(s_new)
    return (yp.reshape(bp, tp, D_MODEL), ys.reshape(bs, ts, D_MODEL),
            jnp.stack(kp_l), jnp.stack(vp_l), jnp.stack(ks_l), jnp.stack(vs_l),
            jnp.stack(cp_l), jnp.stack(cs_l), jnp.stack(dcp_l), jnp.stack(dcs_l),
            jnp.stack(dsp_l), jnp.stack(dss_l))
```

```python
import functools
import math

import jax
import jax.numpy as jnp
from jax import lax
from jax.experimental import pallas as pl
from jax.experimental.pallas import tpu as pltpu

F32 = jnp.float32
BF16 = jnp.bfloat16

D_MODEL = 1024
DEPTH = 4
N_MIXERS = 3
PAGE_SIZE = 128
ATT_HEADS = 8
ATT_DH = 64
ATT_VD = 128
ROPE_THETA = 10000.0
NEG_INF = -1e30
CONV_WIDTH = 31
DN_HEADS = 8
DN_DK = 128
DN_QKV = 3072
DN_CONV = 4
EPS = 1e-6

LANES = 128
ROW_TILE = 256
NORM_PROJ_TILE = 512
OUT_PROJ_TILE = 2048
ATT_BLOCK = 512
PAGES_PER_STEP = 16
CONV_TILE = 256
CONV_HALO = 32
DN_ROWS = 128
DN_HALO = 8
VMEM_LIMIT = 56 * 1024 * 1024


def _params(n_axes):
    return pltpu.CompilerParams(dimension_semantics=("arbitrary",) * n_axes,
                                vmem_limit_bytes=VMEM_LIMIT)


def _mm(a, b):
    return jnp.dot(a.astype(BF16), b.astype(BF16), preferred_element_type=F32)


def _mm_nt(a, b):
    return lax.dot_general(a.astype(BF16), b.astype(BF16), (((1,), (1,)), ((), ())),
                           preferred_element_type=F32)


def _mm_tn(a, b):
    return lax.dot_general(a.astype(BF16), b.astype(BF16), (((0,), (0,)), ((), ())),
                           preferred_element_type=F32)


def _sigmoid(x):
    return 1.0 / (1.0 + jnp.exp(-x))


def _silu(x):
    return x * _sigmoid(x)


def _softplus(x):
    return jnp.maximum(x, 0.0) + jnp.log(1.0 + jnp.exp(-jnp.abs(x)))


def _rms_rows(x):
    return x * lax.rsqrt(jnp.mean(x * x, axis=-1, keepdims=True) + EPS)


def _norm_proj_kernel(x_ref, g_ref, w_ref, o_ref, *, col_chunk):
    h = (_rms_rows(x_ref[...]) * g_ref[...]).astype(BF16)
    n_out = o_ref.shape[1]
    for c0 in range(0, n_out, col_chunk):
        c1 = min(c0 + col_chunk, n_out)
        o_ref[:, c0:c1] = jnp.dot(h, w_ref[:, c0:c1], preferred_element_type=F32)


def _norm_proj(x, gain, w_bf16):
    n, n_out = x.shape[0], w_bf16.shape[1]
    tile = min(NORM_PROJ_TILE, n)
    return pl.pallas_call(
        functools.partial(_norm_proj_kernel, col_chunk=1024),
        grid=(n // tile,),
        in_specs=[pl.BlockSpec((tile, D_MODEL), lambda i: (i, 0)),
                  pl.BlockSpec((1, D_MODEL), lambda i: (0, 0)),
                  pl.BlockSpec((D_MODEL, n_out), lambda i: (0, 0))],
        out_specs=pl.BlockSpec((tile, n_out), lambda i: (i, 0)),
        out_shape=jax.ShapeDtypeStruct((n, n_out), F32),
        compiler_params=_params(1),
        name="norm_proj",
    )(x, gain.reshape(1, D_MODEL), w_bf16)


def _attn_proj_kernel(x_ref, g_ref, w_ref, qn_ref, kn_ref, cos_ref, sin_ref,
                      q_ref, k_ref, v_ref, z_ref, kb_ref, vb_ref):
    h = (_rms_rows(x_ref[...]) * g_ref[...]).astype(BF16)
    width = ATT_HEADS * ATT_VD
    pair = 2 * LANES
    r = lax.broadcasted_iota(jnp.int32, (pair, pair), 0)
    c = lax.broadcasted_iota(jnp.int32, (pair, pair), 1)
    group_ones = ((r >> 6) == (c >> 6)).astype(BF16)
    lane = lax.broadcasted_iota(jnp.int32, (1, LANES), 1)
    first_half = (lane & (ATT_DH - 1)) < (ATT_DH // 2)
    cos = cos_ref[...]
    sin = sin_ref[...]

    def mean_square(raw):
        return jnp.dot((raw * raw).astype(BF16), group_ones, preferred_element_type=F32) * (1.0 / ATT_DH)

    def norm_rope(raw, ms, gain):
        xn = raw * lax.rsqrt(ms + EPS) * gain
        rot = jnp.where(first_half, pltpu.roll(xn, LANES - ATT_DH // 2, 1), pltpu.roll(xn, ATT_DH // 2, 1))
        return xn * cos + rot * sin

    for c0 in range(0, width, pair):
        q_raw = jnp.dot(h, w_ref[:, c0:c0 + pair], preferred_element_type=F32)
        k_raw = jnp.dot(h, w_ref[:, width + c0:width + c0 + pair], preferred_element_type=F32)
        q_ms = mean_square(q_raw)
        k_ms = mean_square(k_raw)
        for s in range(0, pair, LANES):
            sl = slice(c0 + s, c0 + s + LANES)
            q = norm_rope(q_raw[:, s:s + LANES], q_ms[:, s:s + LANES], qn_ref[...]) * (ATT_DH ** -0.5)
            q_ref[:, sl] = q.astype(q_ref.dtype)
            k = norm_rope(k_raw[:, s:s + LANES], k_ms[:, s:s + LANES], kn_ref[...])
            k_ref[:, sl] = k
            kb_ref[:, sl] = k.astype(BF16)
    v = jnp.dot(h, w_ref[:, 2 * width:3 * width], preferred_element_type=F32)
    v_ref[...] = v
    vb_ref[...] = v.astype(BF16)
    z_ref[...] = jnp.dot(h, w_ref[:, 3 * width:4 * width], preferred_element_type=F32)


def _attn_proj(x, gain, w_bf16, qn, kn, cos, sin, q_dtype):
    n = x.shape[0]
    n_tab = cos.shape[0] // ROW_TILE
    width = ATT_HEADS * ATT_VD
    row = lambda i: (i, 0)
    fixed = lambda i: (0, 0)
    out = lambda dt: jax.ShapeDtypeStruct((n, width), dt)
    return pl.pallas_call(
        _attn_proj_kernel,
        grid=(n // ROW_TILE,),
        in_specs=[pl.BlockSpec((ROW_TILE, D_MODEL), row),
                  pl.BlockSpec((1, D_MODEL), fixed),
                  pl.BlockSpec((D_MODEL, 4 * width), fixed),
                  pl.BlockSpec((1, LANES), fixed),
                  pl.BlockSpec((1, LANES), fixed),
                  pl.BlockSpec((ROW_TILE, LANES), lambda i: (i % n_tab, 0)),
                  pl.BlockSpec((ROW_TILE, LANES), lambda i: (i % n_tab, 0))],
        out_specs=[pl.BlockSpec((ROW_TILE, width), row)] * 6,
        out_shape=[out(q_dtype), out(F32), out(F32), out(F32), out(BF16), out(BF16)],
        compiler_params=_params(1),
        name="attn_proj",
    )(x, gain.reshape(1, D_MODEL), w_bf16, jnp.tile(qn, 2).reshape(1, LANES), jnp.tile(kn, 2).reshape(1, LANES),
      cos, sin)


def _rope_tables(pos):
    inv = ROPE_THETA ** (-jnp.arange(0, ATT_DH, 2, dtype=F32) / ATT_DH)
    ang = pos.astype(F32)[:, None] * inv[None, :]
    cos = jnp.cos(ang)
    sin = jnp.sin(ang)
    cos = jnp.concatenate([cos, cos, cos, cos], axis=-1)
    sin = jnp.concatenate([-sin, sin, -sin, sin], axis=-1)
    return cos, sin


def _diff_lambda_in_kernel(lq_ref, lk_ref, lam_init):
    e = jnp.exp(jnp.sum(lq_ref[...] * lk_ref[...], axis=1, keepdims=True))
    return e[0:1] - e[1:2] + lam_init


def _diff_finish(acc1, l1, acc2, l2, lam, lam_init, subln, z):
    o = acc1 / l1 - lam * (acc2 / l2)
    o = _rms_rows(o) * subln * (1.0 - lam_init)
    return o * _silu(z)


def _flash_kernel(q_ref, k_ref, v_ref, z_ref, lq_ref, lk_ref, sub_ref, o_ref, *, lam_init):
    blk = ATT_BLOCK
    nblk = q_ref.shape[0] // blk
    lam = _diff_lambda_in_kernel(lq_ref, lk_ref, lam_init)
    lane = lax.broadcasted_iota(jnp.int32, (1, LANES), 1)
    kv_i = lax.broadcasted_iota(jnp.int32, (blk, 2 * blk), 0)
    q_i = lax.broadcasted_iota(jnp.int32, (blk, 2 * blk), 1) & (blk - 1)
    diag_visible = q_i >= kv_i
    q_maps = {}

    def scores(qi, ki):
        if qi not in q_maps:
            q = q_ref[qi * blk:(qi + 1) * blk, :]
            zero = jnp.zeros_like(q)
            q_maps[qi] = jnp.concatenate([jnp.where(lane < ATT_DH, q, zero), jnp.where(lane >= ATT_DH, q, zero)],
                                         axis=0)
        return lax.dot_general(k_ref[ki * blk:(ki + 1) * blk, :], q_maps[qi], (((1,), (1,)), ((), ())),
                               preferred_element_type=F32)

    pairs = [(qi, ki) for qi in range(nblk) for ki in range(qi + 1)]
    s_next = scores(*pairs[0])
    for n, (qi, ki) in enumerate(pairs):
        s = s_next
        if n + 1 < len(pairs):
            s_next = scores(*pairs[n + 1])
        if ki == 0:
            m = jnp.full((1, 2 * blk), NEG_INF, F32)
            l = jnp.zeros((1, 2 * blk), F32)
            acc = jnp.zeros((ATT_VD, 2 * blk), F32)
        if ki == qi:
            s = jnp.where(diag_visible, s, NEG_INF)
        m_new = jnp.maximum(m, jnp.max(s, axis=0, keepdims=True))
        alpha = jnp.exp(m - m_new)
        p = jnp.exp(s - m_new)
        l = alpha * l + jnp.sum(p, axis=0, keepdims=True)
        pv = lax.dot_general(v_ref[ki * blk:(ki + 1) * blk, :], p.astype(BF16), (((0,), (0,)), ((), ())),
                             preferred_element_type=F32)
        acc = alpha * acc + pv
        m = m_new
        if ki == qi:
            rows = slice(qi * blk, (qi + 1) * blk)
            o_t = acc[:, :blk] / l[:, :blk] - lam * (acc[:, blk:] / l[:, blk:])
            o_t = (o_t * lax.rsqrt(jnp.mean(o_t * o_t, axis=0, keepdims=True) + EPS) * sub_ref[...]
                   * (1.0 - lam_init))
            o_ref[rows, :] = (o_t.T * _silu(z_ref[rows, :])).astype(o_ref.dtype)


def _flash_attention(qb, kb, vb, z, lq, lk, subln, lam_init, batch, seq):
    n = batch * seq
    per_head = lambda b, h: (b, h)
    fixed = lambda b, h: (0, 0)
    return pl.pallas_call(
        functools.partial(_flash_kernel, lam_init=lam_init),
        grid=(batch, ATT_HEADS),
        in_specs=[pl.BlockSpec((seq, LANES), per_head),
                  pl.BlockSpec((seq, LANES), per_head),
                  pl.BlockSpec((seq, LANES), per_head),
                  pl.BlockSpec((seq, LANES), per_head),
                  pl.BlockSpec((2, ATT_DH), fixed),
                  pl.BlockSpec((2, ATT_DH), fixed),
                  pl.BlockSpec((ATT_VD, 1), fixed)],
        out_specs=pl.BlockSpec((seq, LANES), per_head),
        out_shape=jax.ShapeDtypeStruct((n, ATT_HEADS * ATT_VD), BF16),
        compiler_params=_params(2),
        name="flash_diff_attention",
    )(qb, kb, vb, z, lq, lk, subln.reshape(ATT_VD, 1))


def _sample_attn_kernel(*refs, lam_init, n_steps):
    g = PAGES_PER_STEP
    q_ref = refs[1]
    kt_refs = refs[2:2 + g]
    v_refs = refs[2 + g:2 + 2 * g]
    kn_ref, vn_ref, z_ref, lq_ref, lk_ref, sub_ref, o_ref, qbd_sc, m_sc, l_sc, acc_sc = refs[2 + 2 * g:]
    p = pl.program_id(1)
    t = q_ref.shape[0]
    width = q_ref.shape[1]
    n_rows = 2 * ATT_HEADS * t
    per_head = 2 * t

    @pl.when(p == 0)
    def _():
        q = jnp.concatenate([q_ref[...]] * (2 * ATT_HEADS), axis=0)
        r = lax.broadcasted_iota(jnp.int32, (n_rows, width), 0)
        c = lax.broadcasted_iota(jnp.int32, (n_rows, width), 1)
        qbd_sc[...] = jnp.where((r >> 3) == (c >> 6), q, 0.0).astype(BF16)
        m_sc[...] = jnp.full(m_sc.shape, NEG_INF, F32)
        l_sc[...] = jnp.zeros(l_sc.shape, F32)
        acc_sc[...] = jnp.zeros(acc_sc.shape, F32)

    def update(s, v_of_head):
        m_old = m_sc[...]
        m_new = jnp.maximum(m_old, jnp.max(s, axis=1, keepdims=True))
        alpha = jnp.exp(m_old - m_new)
        e = jnp.exp(s - m_new)
        l_sc[...] = alpha * l_sc[...] + jnp.sum(e, axis=1, keepdims=True)
        m_sc[...] = m_new
        e = e.astype(BF16)
        for h in range(ATT_HEADS):
            rows = slice(h * per_head, (h + 1) * per_head)
            acc_sc[rows, :] = alpha[rows] * acc_sc[rows, :] + jnp.dot(e[rows], v_of_head(h),
                                                                      preferred_element_type=F32)

    halves = [slice(0, g // 2), slice(g // 2, g)]
    scores = []
    for half in halves:
        kt = jnp.concatenate([r[...].astype(BF16) for r in kt_refs[half]], axis=1)
        scores.append(jnp.dot(qbd_sc[...], kt, preferred_element_type=F32))
    for half, s in zip(halves, scores):
        def cached_v(h, refs_=v_refs[half]):
            return jnp.concatenate([r[pl.ds(h, PAGE_SIZE, stride=ATT_HEADS), :].astype(BF16) for r in refs_],
                                   axis=0)

        update(s, cached_v)

    @pl.when(p == n_steps - 1)
    def _():
        pad = n_rows - t
        kn = jnp.concatenate([kn_ref[...], jnp.zeros((pad, width), F32)], axis=0).astype(BF16)
        s = lax.dot_general(qbd_sc[...], kn, (((1,), (1,)), ((), ())), preferred_element_type=F32)
        row = lax.broadcasted_iota(jnp.int32, (n_rows, n_rows), 0)
        col = lax.broadcasted_iota(jnp.int32, (n_rows, n_rows), 1)
        s = jnp.where(col <= (row & (t - 1)), s, NEG_INF)

        def new_v(h):
            vh = vn_ref[:, h * ATT_VD:(h + 1) * ATT_VD]
            return jnp.concatenate([vh, jnp.zeros((pad, ATT_VD), F32)], axis=0).astype(BF16)

        update(s, new_v)
        lam = _diff_lambda_in_kernel(lq_ref, lk_ref, lam_init)
        acc = acc_sc[...]
        l = l_sc[...]
        for h in range(ATT_HEADS):
            r0 = h * per_head
            cols = slice(h * ATT_VD, (h + 1) * ATT_VD)
            o_ref[:, cols] = _diff_finish(acc[r0:r0 + t], l[r0:r0 + t], acc[r0 + t:r0 + 2 * t],
                                          l[r0 + t:r0 + 2 * t], lam, lam_init, sub_ref[...], z_ref[:, cols])


def _sample_attention(q, k_new, v_new, z, cache_k, cache_v, layer, page_table, lq, lk, subln, lam_init):
    batch, n_pages = page_table.shape
    t = q.shape[0] // batch
    h, dh, vd = ATT_HEADS, ATT_DH, ATT_VD
    width = h * vd
    g = PAGES_PER_STEP
    n_steps = n_pages // g
    assert t == 8 and n_pages % g == 0 and cache_k.shape[2] == PAGE_SIZE
    n_layer, n_pool = cache_k.shape[:2]
    ckt = cache_k.transpose(0, 1, 3, 4, 5, 2).reshape(n_layer, n_pool, width, PAGE_SIZE)
    cv = cache_v.reshape(n_layer, n_pool, PAGE_SIZE * h, vd)
    pt = page_table.reshape(-1)

    def page(i):
        return lambda b, p, pt_ref: (layer, pt_ref[b * n_pages + p * g + i], 0, 0)

    per_b = lambda b, p, pt_ref: (b, 0)
    fixed = lambda b, p, pt_ref: (0, 0)
    grid_spec = pltpu.PrefetchScalarGridSpec(
        num_scalar_prefetch=1,
        grid=(batch, n_steps),
        in_specs=([pl.BlockSpec((t, width), per_b)]
                  + [pl.BlockSpec((None, None, width, PAGE_SIZE), page(i)) for i in range(g)]
                  + [pl.BlockSpec((None, None, PAGE_SIZE * h, vd), page(i)) for i in range(g)]
                  + [pl.BlockSpec((t, width), per_b),
                     pl.BlockSpec((t, width), per_b),
                     pl.BlockSpec((t, width), per_b),
                     pl.BlockSpec((2, dh), fixed),
                     pl.BlockSpec((2, dh), fixed),
                     pl.BlockSpec((1, vd), fixed)]),
        out_specs=pl.BlockSpec((t, width), per_b),
        scratch_shapes=[pltpu.VMEM((2 * h * t, width), BF16),
                        pltpu.VMEM((2 * h * t, 1), F32),
                        pltpu.VMEM((2 * h * t, 1), F32),
                        pltpu.VMEM((2 * h * t, vd), F32)],
    )
    return pl.pallas_call(
        functools.partial(_sample_attn_kernel, lam_init=lam_init, n_steps=n_steps),
        grid_spec=grid_spec,
        out_shape=jax.ShapeDtypeStruct((batch * t, width), F32),
        compiler_params=_params(2),
        name="paged_diff_attention",
    )(pt, q, *([ckt] * g), *([cv] * g), k_new, v_new, z, lq, lk, subln.reshape(1, vd))


def _out_proj_kernel(g_ref, w_ref, x_ref, y_ref):
    y_ref[...] = x_ref[...] + jnp.dot(g_ref[...].astype(BF16), w_ref[...], preferred_element_type=F32)


def _out_proj(g, w_bf16, x):
    n = x.shape[0]
    tile = min(OUT_PROJ_TILE, n)
    row = lambda i: (i, 0)
    return pl.pallas_call(
        _out_proj_kernel,
        grid=(n // tile,),
        in_specs=[pl.BlockSpec((tile, g.shape[1]), row),
                  pl.BlockSpec(w_bf16.shape, lambda i: (0, 0)),
                  pl.BlockSpec((tile, D_MODEL), row)],
        out_specs=pl.BlockSpec((tile, D_MODEL), row),
        out_shape=jax.ShapeDtypeStruct((n, D_MODEL), F32),
        compiler_params=_params(1),
        name="out_proj",
    )(g, w_bf16, x)


def _conformer_kernel(*refs, has_state):
    if has_state:
        (a_ref, b_ref, z_ref, x_ref, st_ref, dw_ref, db_ref, lg_ref, lb_ref, w_ref,
         y_ref, nb_ref, ubuf) = refs
    else:
        (a_ref, b_ref, z_ref, x_ref, dw_ref, db_ref, lg_ref, lb_ref, w_ref,
         y_ref, nb_ref, ubuf) = refs
    ti = pl.program_id(1)
    nseq, tt, ch = a_ref.shape
    past = CONV_WIDTH - 1
    lead = CONV_HALO - past

    @pl.when(ti == 0)
    def _():
        ubuf[:, 0:CONV_HALO, :] = jnp.zeros((nseq, CONV_HALO, ch), F32)
        if has_state:
            ubuf[:, lead:CONV_HALO, :] = st_ref[...]

    @pl.when(ti > 0)
    def _():
        ubuf[:, 0:CONV_HALO, :] = ubuf[:, tt:tt + CONV_HALO, :]

    ubuf[:, CONV_HALO:CONV_HALO + tt, :] = a_ref[...] * _sigmoid(b_ref[...])
    sub = 8
    acc = jnp.zeros((nseq, tt, ch), F32) + db_ref[...]
    for s in range(sub):
        taps = [j for j in range(CONV_WIDTH) if (lead + j) % sub == s]
        part = None
        n_win = tt + (sub if s else 0)
        for j in taps:
            base = lead + j - s
            term = dw_ref[j] * ubuf[:, base:base + n_win, :]
            part = term if part is None else part + term
        acc = acc + part[:, s:s + tt, :]
    mean = jnp.mean(acc, axis=-1, keepdims=True)
    cen = acc - mean
    var = jnp.mean(cen * cen, axis=-1, keepdims=True)
    c = _silu(cen * lax.rsqrt(var + EPS) * lg_ref[...] + lb_ref[...])
    gated = (c * _silu(z_ref[...])).reshape(nseq * tt, ch).astype(BF16)
    y = jnp.dot(gated, w_ref[...], preferred_element_type=F32)
    y_ref[...] = x_ref[...] + y.reshape(nseq, tt, ch)

    @pl.when(ti == pl.num_programs(1) - 1)
    def _():
        nb_ref[...] = ubuf[:, tt + lead:tt + CONV_HALO, :]


def _conformer(proj, x, state, dw_w, dw_b, ln_g, ln_b, w_out_bf16, nseq, tt):
    bsz, t, _ = x.shape
    ch = D_MODEL
    past = CONV_WIDTH - 1
    col = lambda j: (lambda bi, ti: (bi, ti, j))
    fixed2 = lambda bi, ti: (0, 0)
    in_specs = [pl.BlockSpec((nseq, tt, ch), col(0)),
                pl.BlockSpec((nseq, tt, ch), col(1)),
                pl.BlockSpec((nseq, tt, ch), col(2)),
                pl.BlockSpec((nseq, tt, ch), col(0))]
    args = [proj, proj, proj, x]
    if state is not None:
        in_specs.append(pl.BlockSpec((nseq, past, ch), lambda bi, ti: (bi, 0, 0)))
        args.append(state)
    in_specs += [pl.BlockSpec((CONV_WIDTH, 1, ch), lambda bi, ti: (0, 0, 0)),
                 pl.BlockSpec((1, ch), fixed2),
                 pl.BlockSpec((1, ch), fixed2),
                 pl.BlockSpec((1, ch), fixed2),
                 pl.BlockSpec((ch, D_MODEL), fixed2)]
    args += [dw_w.reshape(CONV_WIDTH, 1, ch), dw_b.reshape(1, ch), ln_g.reshape(1, ch), ln_b.reshape(1, ch),
             w_out_bf16]
    return pl.pallas_call(
        functools.partial(_conformer_kernel, has_state=state is not None),
        grid=(bsz // nseq, t // tt),
        in_specs=in_specs,
        out_specs=[pl.BlockSpec((nseq, tt, D_MODEL), col(0)),
                   pl.BlockSpec((nseq, past, ch), lambda bi, ti: (bi, 0, 0))],
        out_shape=[jax.ShapeDtypeStruct((bsz, t, D_MODEL), F32),
                   jax.ShapeDtypeStruct((bsz, past, ch), F32)],
        scratch_shapes=[pltpu.VMEM((nseq, CONV_HALO + tt, ch), F32)],
        compiler_params=_params(2),
        name="conformer_conv",
    )(*args)


def _unit_lower_inverses(ms, eye, n_stage):
    n = eye.shape[0]
    dot = functools.partial(jnp.dot, preferred_element_type=F32)
    powers = [-m for m in ms]
    invs = [eye + p for p in powers]
    powers = [_mm(p, p) for p in powers]
    for _ in range(n_stage - 2):
        both = [_mm(jnp.concatenate([p, inv], axis=0), p) for p, inv in zip(powers, invs)]
        invs = [inv + b[n:] for inv, b in zip(invs, both)]
        powers = [b[:n] for b in both]
    invs = [inv + _mm(inv, p) for inv, p in zip(invs, powers)]
    mh = [m.astype(BF16) for m in ms]
    ml = [(m - h.astype(F32)).astype(BF16) for m, h in zip(ms, mh)]
    ih = [inv.astype(BF16) for inv in invs]
    il = [(inv - h.astype(F32)).astype(BF16) for inv, h in zip(invs, ih)]
    hi = [dot(jnp.concatenate([a, b], axis=0), c) for a, b, c in zip(mh, ml, ih)]
    lo = [dot(a, c) for a, c in zip(mh, il)]
    resid = [eye - inv - (h[:n] + h[n:] + l) for inv, h, l in zip(invs, hi, lo)]
    return [inv + _mm(inv, r) for inv, r in zip(invs, resid)]


def _delta_kernel(*refs, seq_len, has_state):
    if has_state:
        (x_ref, z_ref, ab_ref, abt_ref, cst_ref, s0_ref, cw_ref, hp_ref, on_ref, wo_ref, res_ref,
         y_ref, nc_ref, s_ref, cbuf) = refs
    else:
        (x_ref, z_ref, ab_ref, abt_ref, cw_ref, hp_ref, on_ref, wo_ref, res_ref,
         y_ref, nc_ref, s_ref, cbuf) = refs
    ci = pl.program_id(1)
    rows, width = x_ref.shape
    nseq = rows // seq_len
    past = DN_CONV - 1
    lead = DN_HALO - past
    dk = DN_DK

    @pl.when(ci == 0)
    def _():
        cbuf[:, 0:DN_HALO, :] = jnp.zeros((nseq, DN_HALO, width), F32)
        if has_state:
            cbuf[:, lead:DN_HALO, :] = cst_ref[...]
            s_ref[...] = s0_ref[...]
        else:
            s_ref[...] = jnp.zeros(s_ref.shape, F32)

    @pl.when(ci > 0)
    def _():
        cbuf[:, 0:DN_HALO, :] = cbuf[:, seq_len:seq_len + DN_HALO, :]

    cbuf[:, DN_HALO:DN_HALO + seq_len, :] = x_ref[...].reshape(nseq, seq_len, width)
    conv = jnp.zeros((nseq, seq_len, width), F32)
    for j in range(DN_CONV):
        conv = conv + cw_ref[j] * cbuf[:, lead + j:lead + j + seq_len, :]
    qkv = _silu(conv).reshape(rows, width)

    @pl.when(ci == pl.num_programs(1) - 1)
    def _():
        nc_ref[...] = cbuf[:, seq_len + lead:seq_len + DN_HALO, :]

    pair = 2 * rows
    ri = lax.broadcasted_iota(jnp.int32, (pair, pair), 0)
    cj = lax.broadcasted_iota(jnp.int32, (pair, pair), 1)
    shift = int(math.log2(seq_len))
    head_shift = int(math.log2(rows))
    same = (ri >> shift) == (cj >> shift)
    lower = (ri >= cj) & same
    strict = (ri > cj) & same
    upper = (ri <= cj) & same
    eye = (ri == cj).astype(F32)
    ab = ab_ref[...]
    abt = abt_ref[...]
    hp = hp_ref[...]

    def stack(fn, heads):
        return jnp.concatenate([fn(h) for h in heads], axis=0)

    def block_diag(x):
        r = lax.broadcasted_iota(jnp.int32, (x.shape[0], 2 * dk), 0)
        c = lax.broadcasted_iota(jnp.int32, (x.shape[0], 2 * dk), 1)
        own = ((r >> head_shift) & 1) == (c >> int(math.log2(dk)))
        return jnp.where(own, jnp.concatenate([x, x], axis=1), 0.0)

    def seq_of(n):
        r = lax.broadcasted_iota(jnp.int32, (n, 1), 0)
        return (r & (rows - 1)) >> shift

    first_heads = list(range(0, DN_HEADS, 2))

    def prepare(h0):
        heads = (h0, h0 + 1)
        q = stack(lambda h: qkv[:, h * dk:(h + 1) * dk], heads)
        k = stack(lambda h: qkv[:, (DN_HEADS + h) * dk:(DN_HEADS + h + 1) * dk], heads)
        v = stack(lambda h: qkv[:, (2 * DN_HEADS + h) * dk:(2 * DN_HEADS + h + 1) * dk], heads)
        q = q * lax.rsqrt(jnp.sum(q * q, axis=-1, keepdims=True) + EPS) * (dk ** -0.5)
        k = k * lax.rsqrt(jnp.sum(k * k, axis=-1, keepdims=True) + EPS)
        neg_a = lambda h: -jnp.exp(hp[0:1, h:h + 1])
        dt_bias = lambda h: hp[1:2, h:h + 1]
        g_col = stack(lambda h: neg_a(h) * _softplus(ab[:, h:h + 1] + dt_bias(h)), heads)
        g_row = jnp.concatenate([neg_a(h) * _softplus(abt[h:h + 1, :] + dt_bias(h)) for h in heads], axis=1)
        beta = stack(lambda h: _sigmoid(ab[:, DN_HEADS + h:DN_HEADS + h + 1]), heads)
        gc_col = jnp.sum(jnp.where(lower, g_row, 0.0), axis=1, keepdims=True)
        gc_row = jnp.sum(jnp.where(upper, g_col, 0.0), axis=0, keepdims=True)
        g_tot = jnp.sum(jnp.where(same, g_row, 0.0), axis=1, keepdims=True)
        decay = jnp.where(lower, jnp.exp(jnp.where(lower, gc_col - gc_row, 0.0)), 0.0)
        kb = k * beta
        kbq = jnp.concatenate([kb, q], axis=0)
        rhs = jnp.concatenate([v * beta, kb * jnp.exp(gc_col)], axis=1)
        qg = q * jnp.exp(gc_col)
        kd = block_diag(k * jnp.exp(g_tot - gc_col))
        return dict(k=k, kbq=kbq, rhs=rhs, qg=qg, kd=kd, decay=decay, e_tot=jnp.exp(g_tot))

    pre = [prepare(h0) for h0 in first_heads]
    kk_qk = [_mm_nt(p["kbq"], p["k"]) for p in pre]
    ms = [jnp.where(strict, x[:pair] * p["decay"], 0.0) for x, p in zip(kk_qk, pre)]
    qks = [x[pair:] * p["decay"] for x, p in zip(kk_qk, pre)]
    tinvs = _unit_lower_inverses(ms, eye, shift)
    u_ws = [_mm(t, p["rhs"]) for t, p in zip(tinvs, pre)]
    us = [x[:, :dk] for x in u_ws]
    w_qgs = [block_diag(jnp.concatenate([x[:, dk:], p["qg"]], axis=0)) for x, p in zip(u_ws, pre)]
    kds = [p["kd"] for p in pre]
    e_tots = [p["e_tot"] for p in pre]

    def state_rows(e_of_head):
        return jnp.concatenate([jnp.broadcast_to(e_of_head(i), (dk, 1)) for i in range(2)], axis=0)

    def load_state(s, h0):
        return s_ref[s, h0:h0 + 2].reshape(2 * dk, dk)

    if nseq == 1:
        s_olds = [load_state(0, h0) for h0 in first_heads]
        ws_qs = [_mm(a, s_old) for a, s_old in zip(w_qgs, s_olds)]
        v_news = [u - x[:pair] for u, x in zip(us, ws_qs)]
        os_ = [x[pair:] + _mm(qk, v_new) for x, qk, v_new in zip(ws_qs, qks, v_news)]
        for h0, s_old, e_tot, kd, v_new in zip(first_heads, s_olds, e_tots, kds, v_news):
            scale = state_rows(lambda i: e_tot[i * rows:i * rows + 1])
            s_ref[0, h0:h0 + 2] = (s_old * scale + _mm_tn(kd, v_new)).reshape(2, dk, dk)
    else:
        seq2 = seq_of(2 * pair)
        seq1 = seq_of(pair)
        head1 = lax.broadcasted_iota(jnp.int32, (pair, 1), 0) >> head_shift

        def read_state(s, accs):
            return tuple(acc + _mm(jnp.where(seq2 == s, a, 0.0), load_state(s, h0))
                         for acc, a, h0 in zip(accs, w_qgs, first_heads))

        ws_qs = lax.fori_loop(0, nseq, read_state,
                              tuple(jnp.zeros((2 * pair, dk), F32) for _ in first_heads))
        v_news = [u - x[:pair] for u, x in zip(us, ws_qs)]
        os_ = [x[pair:] + _mm(qk, v_new) for x, qk, v_new in zip(ws_qs, qks, v_news)]

        def write_state(s, carry):
            mine = seq1 == s
            for h0, e_tot, kd, v_new in zip(first_heads, e_tots, kds, v_news):
                scale = state_rows(
                    lambda i: jnp.max(jnp.where(mine & (head1 == i), e_tot, 0.0), axis=0, keepdims=True))
                s_new = load_state(s, h0) * scale + _mm_tn(jnp.where(mine, kd, 0.0), v_new)
                s_ref[s, h0:h0 + 2] = s_new.reshape(2, dk, dk)
            return carry

        lax.fori_loop(0, nseq, write_state, 0)

    gated = []
    for h0, o in zip(first_heads, os_):
        o = _rms_rows(o) * on_ref[...]
        for i in range(2):
            cols = slice((h0 + i) * dk, (h0 + i + 1) * dk)
            gated.append((o[i * rows:(i + 1) * rows] * _silu(z_ref[:, cols])).astype(BF16))
    y_ref[...] = res_ref[...] + jnp.dot(jnp.concatenate(gated, axis=1), wo_ref[...], preferred_element_type=F32)


def _gated_delta(proj, conv_state, s0, conv_w, a_log, dt_bias, o_norm, w_out_bf16, x, bsz, t, seq_len):
    n = bsz * t
    rows = DN_ROWS
    nseq = rows // seq_len
    n_chunk = max(t // rows, 1)
    n_grp = n // (rows * n_chunk)
    past = DN_CONV - 1
    has_state = conv_state is not None
    ab = proj[:, DN_QKV + D_MODEL:DN_QKV + D_MODEL + 2 * DN_HEADS]
    abt = ab.reshape(n // rows, rows, 2 * DN_HEADS).transpose(0, 2, 1)
    hp = jnp.zeros((2, LANES), F32).at[0, :DN_HEADS].set(a_log).at[1, :DN_HEADS].set(dt_bias)
    blk = lambda j: (lambda gi, ci: (gi * n_chunk + ci, j))
    per_grp3 = lambda gi, ci: (gi, 0, 0)
    per_grp4 = lambda gi, ci: (gi, 0, 0, 0)
    fixed2 = lambda gi, ci: (0, 0)
    in_specs = [pl.BlockSpec((rows, DN_QKV), blk(0)),
                pl.BlockSpec((rows, D_MODEL), blk(DN_QKV // D_MODEL)),
                pl.BlockSpec((rows, LANES), blk((DN_QKV + D_MODEL) // LANES)),
                pl.BlockSpec((None, 2 * DN_HEADS, rows), lambda gi, ci: (gi * n_chunk + ci, 0, 0))]
    args = [proj, proj, proj, abt]
    if has_state:
        in_specs += [pl.BlockSpec((nseq, past, DN_QKV), per_grp3),
                     pl.BlockSpec((nseq, DN_HEADS, DN_DK, DN_DK), per_grp4)]
        args += [conv_state, s0]
    in_specs += [pl.BlockSpec((DN_CONV, 1, DN_QKV), lambda gi, ci: (0, 0, 0)),
                 pl.BlockSpec((2, LANES), fixed2),
                 pl.BlockSpec((1, LANES), fixed2),
                 pl.BlockSpec((D_MODEL, D_MODEL), fixed2),
                 pl.BlockSpec((rows, D_MODEL), blk(0))]
    args += [conv_w.reshape(DN_CONV, 1, DN_QKV), hp, o_norm.reshape(1, LANES), w_out_bf16, x]
    n_state = n_grp * nseq
    return pl.pallas_call(
        functools.partial(_delta_kernel, seq_len=seq_len, has_state=has_state),
        grid=(n_grp, n_chunk),
        in_specs=in_specs,
        out_specs=[pl.BlockSpec((rows, D_MODEL), blk(0)),
                   pl.BlockSpec((nseq, past, DN_QKV), per_grp3),
                   pl.BlockSpec((nseq, DN_HEADS, DN_DK, DN_DK), per_grp4)],
        out_shape=[jax.ShapeDtypeStruct((n, D_MODEL), F32),
                   jax.ShapeDtypeStruct((n_state, past, DN_QKV), F32),
                   jax.ShapeDtypeStruct((n_state, DN_HEADS, DN_DK, DN_DK), F32)],
        scratch_shapes=[pltpu.VMEM((nseq, DN_HALO + seq_len, DN_QKV), F32)],
        compiler_params=_params(2),
        name="gated_delta",
    )(*args)


def kernel(x_prompt, x_sample, cache_k, cache_v, page_table, state_conv, state_delta_conv, state_delta_S, norm_gain, attn_w_in, attn_q_norm, attn_k_norm, attn_lambda_q, attn_lambda_k, attn_subln, attn_w_out, conv_w_in, conv_dw_w, conv_dw_b, conv_ln_g, conv_ln_b, conv_w_out, delta_w_in, delta_conv_w, delta_a_log, delta_dt_bias, delta_o_norm, delta_w_out):
    bp, tp, _ = x_prompt.shape
    bs, ts, _ = x_sample.shape
    past_len = page_table.shape[1] * PAGE_SIZE
    yp = x_prompt.reshape(bp * tp, D_MODEL)
    ys = x_sample.reshape(bs * ts, D_MODEL)

    cos_p, sin_p = _rope_tables(jnp.arange(tp))
    cos_s, sin_s = _rope_tables(past_len + jnp.arange(ts))
    cos_s = jnp.tile(cos_s, (ROW_TILE // ts, 1))
    sin_s = jnp.tile(sin_s, (ROW_TILE // ts, 1))

    kp_l, vp_l, ks_l, vs_l = [], [], [], []
    cp_l, cs_l = [], []
    dcp_l, dcs_l, dsp_l, dss_l = [], [], [], []
    for i in range(DEPTH):
        kind = i % N_MIXERS
        j = i // N_MIXERS
        gain = norm_gain[i]
        if kind == 0:
            lam_init = 0.8 - 0.6 * math.exp(-0.3 * i)
            w_in = attn_w_in[j].astype(BF16)
            w_out = attn_w_out[j].astype(BF16)
            lq, lk = attn_lambda_q[j], attn_lambda_k[j]
            qb, k, v, z, kb, vb = _attn_proj(yp, gain, w_in, attn_q_norm[j], attn_k_norm[j], cos_p, sin_p, BF16)
            gp = _flash_attention(qb, kb, vb, z, lq, lk, attn_subln[j], lam_init, bp, tp)
            kp_l.append(k.reshape(bp, tp, ATT_HEADS, 2, ATT_DH))
            vp_l.append(v.reshape(bp, tp, ATT_HEADS, ATT_VD))
            yp = _out_proj(gp, w_out, yp)
            q, k, v, z, _, _ = _attn_proj(ys, gain, w_in, attn_q_norm[j], attn_k_norm[j], cos_s, sin_s, F32)
            gs = _sample_attention(q, k, v, z, cache_k, cache_v, j, page_table, lq, lk, attn_subln[j], lam_init)
            ks_l.append(k.reshape(bs, ts, ATT_HEADS, 2, ATT_DH))
            vs_l.append(v.reshape(bs, ts, ATT_HEADS, ATT_VD))
            ys = _out_proj(gs, w_out, ys)
        elif kind == 1:
            w_in = conv_w_in[j].astype(BF16)
            w_out = conv_w_out[j].astype(BF16)
            tail = (conv_dw_w[j], conv_dw_b[j], conv_ln_g[j], conv_ln_b[j], w_out)
            proj = _norm_proj(yp, gain, w_in).reshape(bp, tp, 3 * D_MODEL)
            y3, buf = _conformer(proj, yp.reshape(bp, tp, D_MODEL), None, *tail, nseq=1, tt=CONV_TILE)
            yp = y3.reshape(bp * tp, D_MODEL)
            cp_l.append(buf)
            proj = _norm_proj(ys, gain, w_in).reshape(bs, ts, 3 * D_MODEL)
            y3, buf = _conformer(proj, ys.reshape(bs, ts, D_MODEL), state_conv[j], *tail,
                                 nseq=DN_ROWS // ts, tt=ts)
            ys = y3.reshape(bs * ts, D_MODEL)
            cs_l.append(buf)
        else:
            n_in = delta_w_in.shape[2]
            n_pad = -n_in % LANES
            w_in = jnp.pad(delta_w_in[j], ((0, 0), (0, n_pad))).astype(BF16)
            w_out = delta_w_out[j].astype(BF16)
            tail = (delta_conv_w[j], delta_a_log[j], delta_dt_bias[j], delta_o_norm[j])
            proj = _norm_proj(yp, gain, w_in)
            yp, buf, s_new = _gated_delta(proj, None, None, *tail, w_out, yp, bsz=bp, t=tp, seq_len=DN_ROWS)
            dcp_l.append(buf)
            dsp_l.append(s_new)
            proj = _norm_proj(ys, gain, w_in)
            ys, buf, s_new = _gated_delta(proj, state_delta_conv[j], state_delta_S[j], *tail, w_out, ys,
                                          bsz=bs, t=ts, seq_len=ts)
            dcs_l.append(buf)
            dss_l.append(s_new)
    return (yp.reshape(bp, tp, D_MODEL), ys.reshape(bs, ts, D_MODEL),
            jnp.stack(kp_l), jnp.stack(vp_l), jnp.stack(ks_l), jnp.stack(vs_l),
            jnp.stack(cp_l), jnp.stack(cs_l), jnp.stack(dcp_l), jnp.stack(dcs_l),
            jnp.stack(dsp_l), jnp.stack(dss_l))
```
